```python
import jax, jax.numpy as jnp
from jax import lax
import numpy as np

D_MODEL = 1024
BATCH = 4
SEQ = 4096
DEPTH = 1
DEC_BATCH = 128
DEC_SEQ = 4
PAST_LEN = 16384
PAGE_SIZE = 128

H_MLA = 8
Q_LORA = 384
KV_LORA = 256
D_NOPE = 64
D_ROPE = 32
D_V_MLA = 64
ROPE_THETA = 10000.0
MLA_SCALE = (D_NOPE + D_ROPE) ** -0.5
H_FOX = 8
D_FOX = 64
FOX_SCALE = D_FOX ** -0.5
FORGET_BIAS_INIT = 3.0
D_MIX = H_MLA * D_V_MLA + H_FOX * D_FOX
D_FF = 2816
Q_BLOCK = 128
EPS = 1e-6
C_QA = Q_LORA
C_KVA = KV_LORA
C_KR = D_ROPE
C_FOX = H_FOX * D_FOX
C_F = H_FOX
IN_COLS = C_QA + C_KVA + C_KR + 3 * C_FOX + C_F

kernel_name = "hymba_mla_fox_macaron_sandwich_step"


def _rmsnorm(x, g):
    xf = x.astype(jnp.float32)
    y = xf * lax.rsqrt(jnp.mean(xf * xf, axis=-1, keepdims=True) + EPS)
    return (y * g.astype(jnp.float32)).astype(x.dtype)


def _swiglu(x, w_gu, w_down):
    a, b = jnp.split(x @ w_gu, 2, axis=-1)
    return (jax.nn.silu(a) * b) @ w_down


def _rope(x, pos):
    half = D_ROPE // 2
    freq = ROPE_THETA ** (-jnp.arange(half, dtype=jnp.float32) * 2.0 / D_ROPE)
    ang = pos[:, None] * freq[None, :]
    shape = (pos.shape[0],) + (1,) * (x.ndim - 3) + (half,)
    cos = jnp.cos(ang).reshape(shape)
    sin = jnp.sin(ang).reshape(shape)
    xf = x.astype(jnp.float32)
    x1, x2 = xf[..., :half], xf[..., half:]
    return jnp.concatenate([x1 * cos - x2 * sin, x2 * cos + x1 * sin], axis=-1).astype(x.dtype)


def _mla_latent_attend(q_lat, q_rope, c_kv, k_rope, q_pos, k_pos):
    logits = (jnp.einsum("bqhc,bkc->bhqk", q_lat, c_kv).astype(jnp.float32)
              + jnp.einsum("bqhr,bkr->bhqk", q_rope, k_rope).astype(jnp.float32)) * MLA_SCALE
    logits = jnp.where(k_pos[None, :] <= q_pos[:, None], logits, -jnp.inf)
    p = jax.nn.softmax(logits, axis=-1)
    return jnp.einsum("bhqk,bkc->bqhc", p.astype(c_kv.dtype), c_kv)


def _fox_attend(q, k, v, cq, ck, q_pos, k_pos):
    logits = jnp.einsum("bqhd,bkhd->bhqk", q, k).astype(jnp.float32) * FOX_SCALE
    logits = logits + (jnp.transpose(cq, (0, 2, 1))[..., :, None] - jnp.transpose(ck, (0, 2, 1))[..., None, :])
    logits = jnp.where(k_pos[None, :] <= q_pos[:, None], logits, -jnp.inf)
    p = jax.nn.softmax(logits, axis=-1)
    return jnp.einsum("bhqk,bkhd->bqhd", p.astype(v.dtype), v)


def _mix_inputs(h, prm, pos):
    B, T, _ = h.shape
    proj = h @ prm["w_in"]
    o1 = C_QA
    o2 = o1 + C_KVA
    o3 = o2 + C_KR
    o4 = o3 + C_FOX
    o5 = o4 + C_FOX
    o6 = o5 + C_FOX
    q_a, kv_a, kr_raw = proj[..., :o1], proj[..., o1:o2], proj[..., o2:o3]
    fq = proj[..., o3:o4].reshape(B, T, H_FOX, D_FOX)
    fk = proj[..., o4:o5].reshape(B, T, H_FOX, D_FOX)
    fv = proj[..., o5:o6].reshape(B, T, H_FOX, D_FOX)
    f_logit = proj[..., o6:]
    q = (_rmsnorm(q_a, prm["g_q_a"]) @ prm["w_q_b"]).reshape(B, T, H_MLA, D_NOPE + D_ROPE)
    q_nope, q_rope = q[..., :D_NOPE], _rope(q[..., D_NOPE:], pos)
    w_uk = prm["w_kv_b"].reshape(KV_LORA, H_MLA, D_NOPE + D_V_MLA)[..., :D_NOPE]
    q_lat = jnp.einsum("bthd,chd->bthc", q_nope, w_uk)
    c_kv = _rmsnorm(kv_a, prm["g_kv_a"])
    k_rope = _rope(kr_raw, pos)
    logf = jax.nn.log_sigmoid(f_logit.astype(jnp.float32) + prm["b_forget"].astype(jnp.float32))
    return q_lat, q_rope, c_kv, k_rope, fq, fk, fv, logf


def _mix_output(o_lat, o_fox, prm):
    B, T = o_lat.shape[:2]
    w_uv = prm["w_kv_b"].reshape(KV_LORA, H_MLA, D_NOPE + D_V_MLA)[..., D_NOPE:]
    o_mla = jnp.einsum("bthc,chd->bthd", o_lat, w_uv)
    o = jnp.concatenate([o_mla.reshape(B, T, H_MLA * D_V_MLA), o_fox.reshape(B, T, H_FOX * D_FOX)], axis=-1)
    return o @ prm["w_o"]


def _mix_prompt(h, prm):
    T = h.shape[1]
    pos = jnp.arange(T, dtype=jnp.float32)
    q_lat, q_rope, c_kv, k_rope, fq, fk, fv, logf = _mix_inputs(h, prm, pos)
    F = jnp.cumsum(logf, axis=1)
    outs_mla, outs_fox = [], []
    for s in range(0, T, Q_BLOCK):
        e = s + Q_BLOCK
        qp, kp = pos[s:e], pos[:e]
        outs_mla.append(_mla_latent_attend(q_lat[:, s:e], q_rope[:, s:e], c_kv[:, :e], k_rope[:, :e], qp, kp))
        outs_fox.append(_fox_attend(fq[:, s:e], fk[:, :e], fv[:, :e], F[:, s:e], F[:, :e], qp, kp))
    y = _mix_output(jnp.concatenate(outs_mla, axis=1), jnp.concatenate(outs_fox, axis=1), prm)
    return y, (c_kv, k_rope, fk, fv, logf)


def _mix_sample(h, prm, past_lat, past_kr, past_k, past_v, past_lf):
    T = h.shape[1]
    q_pos = PAST_LEN + jnp.arange(T, dtype=jnp.float32)
    k_pos = jnp.arange(PAST_LEN + T, dtype=jnp.float32)
    q_lat, q_rope, c_kv, k_rope, fq, fk, fv, logf = _mix_inputs(h, prm, q_pos)
    c_all = jnp.concatenate([past_lat.astype(c_kv.dtype), c_kv], axis=1)
    kr_all = jnp.concatenate([past_kr.astype(k_rope.dtype), k_rope], axis=1)
    k_all = jnp.concatenate([past_k.astype(fk.dtype), fk], axis=1)
    v_all = jnp.concatenate([past_v.astype(fv.dtype), fv], axis=1)
    lf_all = jnp.concatenate([past_lf.astype(jnp.float32), logf], axis=1)
    G = lax.cumsum(lf_all, axis=1, reverse=True) - lf_all
    o_lat = _mla_latent_attend(q_lat, q_rope, c_all, kr_all, q_pos, k_pos)
    o_fox = _fox_attend(fq, k_all, v_all, -G[:, PAST_LEN:], -G, q_pos, k_pos)
    y = _mix_output(o_lat, o_fox, prm)
    return y, (c_kv, k_rope, fk, fv, logf)


def _ffn_half(x, prm, tag):
    h = _rmsnorm(x, prm["g_pre_" + tag])
    return x + 0.5 * _rmsnorm(_swiglu(h, prm["w_gu_" + tag], prm["w_down_" + tag]), prm["g_post_" + tag])


def _gather_pages(cache_l, page_table):
    g = cache_l[page_table]
    return g.reshape((page_table.shape[0], PAST_LEN) + cache_l.shape[2:])


def setup_inputs(seed: int = 0) -> dict:
    key = jax.random.key(seed)
    ks = jax.random.split(key, 32)
    n_pages = PAST_LEN // PAGE_SIZE
    n_used = DEC_BATCH * n_pages
    n_pool = n_used + n_used // 4

    def nrm(k, shape, scale=1.0):
        return jax.random.normal(k, shape, jnp.float32) * scale

    def gain(k, n):
        return 1.0 + 0.1 * nrm(k, (DEPTH, n))

    page_table = jax.random.permutation(ks[0], n_pool)[:n_used].reshape(DEC_BATCH, n_pages).astype(jnp.int32)
    return {
        "x_prompt": nrm(ks[1], (BATCH, SEQ, D_MODEL)),
        "x_sample": nrm(ks[2], (DEC_BATCH, DEC_SEQ, D_MODEL)),
        "cache_mla_latent": nrm(ks[3], (DEPTH, n_pool, PAGE_SIZE, KV_LORA)),
        "cache_mla_krope": nrm(ks[4], (DEPTH, n_pool, PAGE_SIZE, D_ROPE)),
        "cache_fox_k": nrm(ks[5], (DEPTH, n_pool, PAGE_SIZE, H_FOX, D_FOX)),
        "cache_fox_v": nrm(ks[6], (DEPTH, n_pool, PAGE_SIZE, H_FOX, D_FOX)),
        "cache_fox_logf": jax.nn.log_sigmoid(FORGET_BIAS_INIT + nrm(ks[7], (DEPTH, n_pool, PAGE_SIZE, H_FOX))),
        "page_table": page_table,
        "g_pre_ffn1": gain(ks[8], D_MODEL),
        "w_gu_ffn1": nrm(ks[9], (DEPTH, D_MODEL, 2 * D_FF), D_MODEL ** -0.5),
        "w_down_ffn1": nrm(ks[10], (DEPTH, D_FF, D_MODEL), D_FF ** -0.5),
        "g_post_ffn1": gain(ks[11], D_MODEL),
        "g_pre_mix": gain(ks[12], D_MODEL),
        "w_in": nrm(ks[13], (DEPTH, D_MODEL, IN_COLS), D_MODEL ** -0.5),
        "g_q_a": gain(ks[14], Q_LORA),
        "w_q_b": nrm(ks[15], (DEPTH, Q_LORA, H_MLA * (D_NOPE + D_ROPE)), Q_LORA ** -0.5),
        "g_kv_a": gain(ks[16], KV_LORA),
        "w_kv_b": nrm(ks[17], (DEPTH, KV_LORA, H_MLA * (D_NOPE + D_V_MLA)), KV_LORA ** -0.5),
        "b_forget": FORGET_BIAS_INIT + 0.1 * nrm(ks[18], (DEPTH, H_FOX)),
        "w_o": nrm(ks[19], (DEPTH, D_MIX, D_MODEL), D_MIX ** -0.5),
        "g_post_mix": gain(ks[20], D_MODEL),
        "g_pre_ffn2": gain(ks[21], D_MODEL),
        "w_gu_ffn2": nrm(ks[22], (DEPTH, D_MODEL, 2 * D_FF), D_MODEL ** -0.5),
        "w_down_ffn2": nrm(ks[23], (DEPTH, D_FF, D_MODEL), D_FF ** -0.5),
        "g_post_ffn2": gain(ks[24], D_MODEL),
    }


def reference(x_prompt, x_sample, cache_mla_latent, cache_mla_krope, cache_fox_k, cache_fox_v, cache_fox_logf,
              page_table, g_pre_ffn1, w_gu_ffn1, w_down_ffn1, g_post_ffn1, g_pre_mix, w_in, g_q_a, w_q_b, g_kv_a,
              w_kv_b, b_forget, w_o, g_post_mix, g_pre_ffn2, w_gu_ffn2, w_down_ffn2, g_post_ffn2):
    yp, ys = x_prompt, x_sample
    p_rows = [[], [], [], [], []]
    s_rows = [[], [], [], [], []]
    for l in range(DEPTH):
        prm = {
            "g_pre_ffn1": g_pre_ffn1[l], "w_gu_ffn1": w_gu_ffn1[l], "w_down_ffn1": w_down_ffn1[l],
            "g_post_ffn1": g_post_ffn1[l], "g_pre_mix": g_pre_mix[l], "w_in": w_in[l], "g_q_a": g_q_a[l],
            "w_q_b": w_q_b[l], "g_kv_a": g_kv_a[l], "w_kv_b": w_kv_b[l], "b_forget": b_forget[l],
            "w_o": w_o[l], "g_post_mix": g_post_mix[l], "g_pre_ffn2": g_pre_ffn2[l],
            "w_gu_ffn2": w_gu_ffn2[l], "w_down_ffn2": w_down_ffn2[l], "g_post_ffn2": g_post_ffn2[l],
        }
        yp = _ffn_half(yp, prm, "ffn1")
        m, rows_p = _mix_prompt(_rmsnorm(yp, prm["g_pre_mix"]), prm)
        yp = yp + _rmsnorm(m, prm["g_post_mix"])
        yp = _ffn_half(yp, prm, "ffn2")
        past_lat = _gather_pages(cache_mla_latent[l], page_table)
        past_kr = _gather_pages(cache_mla_krope[l], page_table)
        past_k = _gather_pages(cache_fox_k[l], page_table)
        past_v = _gather_pages(cache_fox_v[l], page_table)
        past_lf = _gather_pages(cache_fox_logf[l], page_table)
        ys = _ffn_half(ys, prm, "ffn1")
        m, rows_s = _mix_sample(_rmsnorm(ys, prm["g_pre_mix"]), prm, past_lat, past_kr, past_k, past_v, past_lf)
        ys = ys + _rmsnorm(m, prm["g_post_mix"])
        ys = _ffn_half(ys, prm, "ffn2")
        for i in range(5):
            p_rows[i].append(rows_p[i])
            s_rows[i].append(rows_s[i])
    p_lat, p_kr, p_k, p_v, p_lf = [jnp.stack(r, axis=0) for r in p_rows]
    s_lat, s_kr, s_k, s_v, s_lf = [jnp.stack(r, axis=0) for r in s_rows]
    return (yp, ys, p_lat, p_kr, p_k, p_v, p_lf, s_lat, s_kr, s_k, s_v, s_lf)
```

```python
import functools

import jax
import jax.numpy as jnp
import numpy as np
from jax import lax
from jax.experimental import pallas as pl
from jax.experimental.pallas import tpu as pltpu

H_MLA = 8
D_NOPE = 64
D_ROPE = 32
D_V_MLA = 64
H_FOX = 8
D_FOX = 64
ROPE_THETA = 10000.0
EPS = 1e-6
MLA_SCALE = (D_NOPE + D_ROPE) ** -0.5
FOX_SCALE = D_FOX ** -0.5
HALF_ROPE = D_ROPE // 2

F32 = jnp.float32
BF16 = jnp.bfloat16
NEG_BIG = -1e30

VMEM_LIMIT_BYTES = 56 * 1024 * 1024
LANES = 128

ROW_TILE = 512
FF_CHUNK = 256
MLA_TQ = 256
MLA_TK = 512
FOX_TQ = 512
FOX_TK = 512
CUMSUM_BLOCK = 256
PAGES_PER_STEP = 8
NEW_PAD = 16


def _dot(a, b):
    return jnp.dot(a, b, preferred_element_type=F32)


def _dot_nt(a, b):
    return lax.dot_general(a, b, (((1,), (1,)), ((), ())), preferred_element_type=F32)


def _rms(x, g):
    return x * lax.rsqrt(jnp.mean(x * x, axis=-1, keepdims=True) + EPS) * g


def _resident(shape):
    nd = len(shape)
    return pl.BlockSpec(shape, lambda *_: (0,) * nd, pipeline_mode=pl.Buffered(1))


def _params(*sem):
    return pltpu.CompilerParams(dimension_semantics=sem, vmem_limit_bytes=VMEM_LIMIT_BYTES)


def _ffn_kernel(x_ref, gpre_ref, wgu_ref, wd_ref, gpost_ref, o_ref, acc_ref, *, d_ff):
    x = x_ref[...]
    h = _rms(x, gpre_ref[...]).astype(BF16)
    for c in range(d_ff // FF_CHUNK):
        lo = c * FF_CHUNK
        a = _dot(h, wgu_ref[:, lo:lo + FF_CHUNK])
        b = _dot(h, wgu_ref[:, d_ff + lo:d_ff + lo + FF_CHUNK])
        act = (a * jax.nn.sigmoid(a) * b).astype(BF16)
        part = _dot(act, wd_ref[lo:lo + FF_CHUNK, :])
        if c == 0:
            acc_ref[...] = part
        else:
            acc_ref[...] += part
    o_ref[...] = x + 0.5 * _rms(acc_ref[...], gpost_ref[...])


def _ffn_half(x, g_pre, w_gu, w_down, g_post):
    n, d = x.shape
    d_ff = w_down.shape[0]
    tm = min(ROW_TILE, n)
    return pl.pallas_call(
        functools.partial(_ffn_kernel, d_ff=d_ff),
        grid=(n // tm,),
        in_specs=[
            pl.BlockSpec((tm, d), lambda i: (i, 0)),
            _resident((1, d)),
            _resident((d, 2 * d_ff)),
            _resident((d_ff, d)),
            _resident((1, d)),
        ],
        out_specs=pl.BlockSpec((tm, d), lambda i: (i, 0)),
        out_shape=jax.ShapeDtypeStruct((n, d), F32),
        scratch_shapes=[pltpu.VMEM((tm, d), F32)],
        compiler_params=_params("parallel"),
        name="ffn_half",
    )(x, g_pre, w_gu, w_down, g_post)


def _log_sigmoid(x):
    return -(jnp.maximum(-x, 0.0) + jnp.log1p(jnp.exp(-jnp.abs(x))))


def _mix_in_kernel(*refs, q_lora, kv_lora, transposed):
    if transposed:
        (y_ref, gpre_ref, wn_ref, wt_ref, gq_ref, wq_ref, gkv_ref, wuk_ref, b_ref, bt_ref,
         cosq_ref, sinq_ref, cost_ref, sint_ref,
         ckv_ref, ckvb_ref, qlat_ref, qrope_ref, fq_ref, lf_ref,
         fkt_ref, fktb_ref, fvt_ref, fvtb_ref, krt_ref, krtb_ref, lft_ref) = refs
    else:
        (y_ref, gpre_ref, wn_ref, gq_ref, wq_ref, gkv_ref, wuk_ref, b_ref,
         cosq_ref, sinq_ref,
         ckv_ref, qlat_ref, qrope_ref, fq_ref, lf_ref, fk_ref, fv_ref, kr_ref) = refs
    c_fox = H_FOX * D_FOX
    o_kv = q_lora
    o_fq = o_kv + kv_lora
    o_rest = o_fq + c_fox

    h = _rms(y_ref[0], gpre_ref[...]).astype(BF16)
    pn = _dot(h, wn_ref[...])

    ckv = _rms(pn[:, o_kv:o_fq], gkv_ref[...])
    ckv_ref[0] = ckv
    if transposed:
        ckvb_ref[0] = ckv.astype(BF16)

    qn = _rms(pn[:, :q_lora], gq_ref[...]).astype(BF16)
    q = _dot(qn, wq_ref[...])
    n_nope = H_MLA * D_NOPE
    for hh in range(H_MLA):
        qh = q[:, hh * D_NOPE:(hh + 1) * D_NOPE].astype(BF16)
        qlat_ref[0, hh] = (_dot(qh, wuk_ref[hh]) * MLA_SCALE).astype(BF16)
    qr = q[:, n_nope:]
    width = H_MLA * D_ROPE
    lane = lax.broadcasted_iota(jnp.int32, qr.shape, 1)
    partner = jnp.where((lane % D_ROPE) < HALF_ROPE,
                        pltpu.roll(qr, width - HALF_ROPE, 1), pltpu.roll(qr, HALF_ROPE, 1))
    qr = (qr * cosq_ref[0] + partner * sinq_ref[0]) * MLA_SCALE
    for hh in range(H_MLA):
        qrope_ref[0, hh] = qr[:, hh * D_ROPE:(hh + 1) * D_ROPE].astype(BF16)

    fq = pn[:, o_fq:o_rest] * FOX_SCALE
    for hh in range(H_FOX):
        fq_ref[0, hh] = fq[:, hh * D_FOX:(hh + 1) * D_FOX].astype(BF16)

    if transposed:
        lf_ref[0] = _log_sigmoid(pn[:, o_rest:o_rest + H_FOX] + b_ref[...])
        pt = _dot_nt(wt_ref[...], h)
        fkt = pt[:c_fox]
        fvt = pt[c_fox:2 * c_fox]
        fkt_ref[0] = fkt
        fktb_ref[0] = fkt.astype(BF16)
        fvt_ref[0] = fvt
        fvtb_ref[0] = fvt.astype(BF16)
        o_kr = 2 * c_fox
        x1 = pt[o_kr:o_kr + HALF_ROPE]
        x2 = pt[o_kr + HALF_ROPE:o_kr + D_ROPE]
        cos, sin = cost_ref[...], sint_ref[...]
        krt = jnp.concatenate([x1 * cos - x2 * sin, x2 * cos + x1 * sin], axis=0)
        krt_ref[0] = krt
        krtb_ref[0] = krt.astype(BF16)
        lft_ref[0] = _log_sigmoid(pt[o_kr + D_ROPE:o_kr + D_ROPE + H_FOX] + bt_ref[...])
    else:
        o_fv = o_rest + c_fox
        o_kr = o_fv + c_fox
        fk_ref[0] = pn[:, o_rest:o_fv]
        fv_ref[0] = pn[:, o_fv:o_kr]
        tail = pn[:, o_kr:o_kr + LANES]
        lane = lax.broadcasted_iota(jnp.int32, tail.shape, 1)
        partner = jnp.where(lane < HALF_ROPE, pltpu.roll(tail, LANES - HALF_ROPE, 1), pltpu.roll(tail, HALF_ROPE, 1))
        kr = tail * cosq_ref[0][:, :LANES] + partner * sinq_ref[0][:, :LANES]
        kr_ref[0] = kr[:, :D_ROPE]
        lf_ref[0] = _log_sigmoid(tail[:, D_ROPE:D_ROPE + H_FOX] + b_ref[...])


def _mix_in(y, prm, rope, *, transposed):
    bsz, t, d = y.shape
    tm = min(ROW_TILE, t)
    q_lora = prm["g_q_a"].shape[1]
    kv_lora = prm["g_kv_a"].shape[1]
    c_fox = H_FOX * D_FOX
    grid = (bsz, t // tm)
    tok = lambda c: pl.BlockSpec((1, tm, c), lambda b, i: (b, i, 0))
    head = lambda c: pl.BlockSpec((1, H_MLA, tm, c), lambda b, i: (b, 0, i, 0))
    chan = lambda c: pl.BlockSpec((1, c, tm), lambda b, i: (b, 0, i))
    sds = jax.ShapeDtypeStruct
    wn = prm["w_in_n_t"] if transposed else prm["w_in_n"]
    ins = [y, prm["g_pre_mix"], wn]
    in_specs = [tok(d), _resident((1, d)), _resident(wn.shape)]
    if transposed:
        ins.append(prm["w_in_t"])
        in_specs.append(_resident(prm["w_in_t"].shape))
    ins += [prm["g_q_a"], prm["w_q"], prm["g_kv_a"], prm["w_uk_t"], prm["b_forget"]]
    in_specs += [_resident((1, q_lora)), _resident(prm["w_q"].shape), _resident((1, kv_lora)),
                 _resident(prm["w_uk_t"].shape), _resident((1, H_FOX))]
    if transposed:
        ins.append(prm["b_forget_t"])
        in_specs.append(_resident((H_FOX, 1)))
    rw = H_MLA * D_ROPE
    ins += [rope["cos_q"], rope["sin_q"]]
    in_specs += [pl.BlockSpec((1, tm, rw), lambda b, i: (0, i, 0))] * 2
    if transposed:
        ins += [rope["cos_t"], rope["sin_t"]]
        in_specs += [pl.BlockSpec((HALF_ROPE, tm), lambda b, i: (0, i))] * 2
    names = ["ckv"]
    out_shape = [sds((bsz, t, kv_lora), F32)]
    out_specs = [tok(kv_lora)]
    if transposed:
        names.append("ckv_b")
        out_shape.append(sds((bsz, t, kv_lora), BF16))
        out_specs.append(tok(kv_lora))
    names += ["q_lat", "q_rope", "fq", "lf"]
    out_shape += [sds((bsz, H_MLA, t, kv_lora), BF16), sds((bsz, H_MLA, t, D_ROPE), BF16),
                  sds((bsz, H_FOX, t, D_FOX), BF16), sds((bsz, t, H_FOX), F32)]
    out_specs += [head(kv_lora), head(D_ROPE), head(D_FOX), tok(H_FOX)]
    if transposed:
        names += ["fk_t", "fk_tb", "fv_t", "fv_tb", "kr_t", "kr_tb", "lf_t"]
        out_shape += [sds((bsz, c_fox, t), F32), sds((bsz, c_fox, t), BF16),
                      sds((bsz, c_fox, t), F32), sds((bsz, c_fox, t), BF16),
                      sds((bsz, D_ROPE, t), F32), sds((bsz, D_ROPE, t), BF16), sds((bsz, H_FOX, t), F32)]
        out_specs += [chan(c_fox)] * 4 + [chan(D_ROPE)] * 2 + [chan(H_FOX)]
    else:
        names += ["fk", "fv", "kr"]
        out_shape += [sds((bsz, t, c_fox), F32), sds((bsz, t, c_fox), F32), sds((bsz, t, D_ROPE), F32)]
        out_specs += [tok(c_fox), tok(c_fox), tok(D_ROPE)]
    outs = pl.pallas_call(
        functools.partial(_mix_in_kernel, q_lora=q_lora, kv_lora=kv_lora, transposed=transposed),
        grid=grid, in_specs=in_specs, out_specs=out_specs, out_shape=out_shape,
        compiler_params=_params("parallel", "parallel"),
        name="mix_in_prompt" if transposed else "mix_in_sample",
    )(*ins)
    return dict(zip(names, outs))


def _cumsum_kernel(lf_ref, lft_ref, f_ref, ft_ref):
    t = lf_ref.shape[1]
    nb = t // CUMSUM_BLOCK
    r = lax.broadcasted_iota(jnp.int32, (CUMSUM_BLOCK, CUMSUM_BLOCK), 0)
    c = lax.broadcasted_iota(jnp.int32, (CUMSUM_BLOCK, CUMSUM_BLOCK), 1)
    tri = (c <= r).astype(F32)
    carry = jnp.zeros((1, H_FOX), F32)
    carry_t = jnp.zeros((H_FOX, 1), F32)
    for i in range(nb):
        lo = i * CUMSUM_BLOCK
        blk = jnp.dot(tri, lf_ref[0, lo:lo + CUMSUM_BLOCK, :], preferred_element_type=F32,
                      precision=lax.Precision.HIGHEST) + carry
        f_ref[0, lo:lo + CUMSUM_BLOCK, :] = blk
        carry = blk[CUMSUM_BLOCK - 1:CUMSUM_BLOCK, :]
        blk_t = lax.dot_general(lft_ref[0, :, lo:lo + CUMSUM_BLOCK], tri, (((1,), (1,)), ((), ())),
                                preferred_element_type=F32, precision=lax.Precision.HIGHEST) + carry_t
        ft_ref[0, :, lo:lo + CUMSUM_BLOCK] = blk_t
        carry_t = blk_t[:, CUMSUM_BLOCK - 1:CUMSUM_BLOCK]


def _fox_cumsum(lf, lf_t):
    bsz, t, hh = lf.shape
    return pl.pallas_call(
        _cumsum_kernel,
        grid=(bsz,),
        in_specs=[pl.BlockSpec((1, t, hh), lambda b: (b, 0, 0)), pl.BlockSpec((1, hh, t), lambda b: (b, 0, 0))],
        out_specs=[pl.BlockSpec((1, t, hh), lambda b: (b, 0, 0)), pl.BlockSpec((1, hh, t), lambda b: (b, 0, 0))],
        out_shape=[jax.ShapeDtypeStruct(lf.shape, F32), jax.ShapeDtypeStruct(lf_t.shape, F32)],
        compiler_params=_params("parallel"),
        name="fox_cumsum",
    )(lf, lf_t)


def _causal_pairs(t, tq, tk):
    qi, ki = [], []
    for i in range(t // tq):
        last = (i * tq + tq - 1) // tk
        for j in range(last + 1):
            qi.append(i)
            ki.append(j)
    return jnp.asarray(qi, jnp.int32), jnp.asarray(ki, jnp.int32)


def _mla_prefill_kernel(qi_ref, ki_ref, qlat_ref, qrope_ref, ckv_ref, krt_ref, wuv_ref, o_ref,
                        m_ref, l_ref, acc_ref, *, tq, tk):
    p_id = pl.program_id(1)
    qi = qi_ref[p_id]
    ki = ki_ref[p_id]
    rows = H_MLA * tq

    @pl.when(ki == 0)
    def _():
        m_ref[...] = jnp.full(m_ref.shape, NEG_BIG, F32)
        l_ref[...] = jnp.zeros(l_ref.shape, F32)
        acc_ref[...] = jnp.zeros(acc_ref.shape, F32)

    def step(masked):
        ckv = ckv_ref[0]
        q_lat = qlat_ref[0].reshape(rows, qlat_ref.shape[-1])
        q_rope = qrope_ref[0].reshape(rows, D_ROPE)
        s = _dot_nt(q_lat, ckv) + _dot(q_rope, krt_ref[0])
        if masked:
            q_pos = qi * tq + lax.broadcasted_iota(jnp.int32, (tq, tk), 0)
            k_pos = ki * tk + lax.broadcasted_iota(jnp.int32, (tq, tk), 1)
            s = jnp.where((k_pos <= q_pos)[None], s.reshape(H_MLA, tq, tk), NEG_BIG).reshape(rows, tk)
        m_old = m_ref[...]
        m_new = jnp.maximum(m_old, jnp.max(s, axis=-1, keepdims=True))
        alpha = jnp.exp(m_old - m_new)
        p = jnp.exp(s - m_new)
        l_ref[...] = alpha * l_ref[...] + jnp.sum(p, axis=-1, keepdims=True)
        acc_ref[...] = alpha * acc_ref[...] + _dot(p.astype(BF16), ckv)
        m_ref[...] = m_new

    needs_mask = (ki + 1) * tk - 1 > qi * tq

    @pl.when(needs_mask)
    def _():
        step(True)

    @pl.when(jnp.logical_not(needs_mask))
    def _():
        step(False)

    @pl.when(ki == (qi * tq + tq - 1) // tk)
    def _():
        o_lat = (acc_ref[...] / l_ref[...]).astype(BF16)
        o_ref[0] = jnp.concatenate(
            [_dot(o_lat[hh * tq:(hh + 1) * tq], wuv_ref[hh]) for hh in range(H_MLA)], axis=-1).astype(BF16)


def _mla_prefill(q_lat, q_rope, ckv_b, kr_tb, w_uv):
    bsz, _, t, c = q_lat.shape
    tq, tk = min(MLA_TQ, t), min(MLA_TK, t)
    qi, ki = _causal_pairs(t, tq, tk)
    rows = H_MLA * tq
    grid_spec = pltpu.PrefetchScalarGridSpec(
        num_scalar_prefetch=2,
        grid=(bsz, qi.shape[0]),
        in_specs=[
            pl.BlockSpec((1, H_MLA, tq, c), lambda b, p, qi, ki: (b, 0, qi[p], 0)),
            pl.BlockSpec((1, H_MLA, tq, D_ROPE), lambda b, p, qi, ki: (b, 0, qi[p], 0)),
            pl.BlockSpec((1, tk, c), lambda b, p, qi, ki: (b, ki[p], 0)),
            pl.BlockSpec((1, D_ROPE, tk), lambda b, p, qi, ki: (b, 0, ki[p])),
            _resident(w_uv.shape),
        ],
        out_specs=pl.BlockSpec((1, tq, H_MLA * D_V_MLA), lambda b, p, qi, ki: (b, qi[p], 0)),
        scratch_shapes=[pltpu.VMEM((rows, 1), F32), pltpu.VMEM((rows, 1), F32), pltpu.VMEM((rows, c), F32)],
    )
    return pl.pallas_call(
        functools.partial(_mla_prefill_kernel, tq=tq, tk=tk),
        grid_spec=grid_spec,
        out_shape=jax.ShapeDtypeStruct((bsz, t, H_MLA * D_V_MLA), BF16),
        compiler_params=_params("parallel", "arbitrary"),
        name="mla_prefill",
    )(qi, ki, q_lat, q_rope, ckv_b, kr_tb, w_uv)


def _fox_prefill_kernel(qi_ref, ki_ref, q_ref, kt_ref, vt_ref, f_ref, ft_ref, o_ref,
                        m_ref, l_ref, acc_ref, *, tq, tk):
    p_id = pl.program_id(1)
    qi = qi_ref[p_id]
    ki = ki_ref[p_id]

    @pl.when(ki == 0)
    def _():
        m_ref[...] = jnp.full(m_ref.shape, NEG_BIG, F32)
        l_ref[...] = jnp.zeros(l_ref.shape, F32)
        acc_ref[...] = jnp.zeros(acc_ref.shape, F32)

    def step(masked):
        if masked:
            q_pos = qi * tq + lax.broadcasted_iota(jnp.int32, (tq, tk), 0)
            k_pos = ki * tk + lax.broadcasted_iota(jnp.int32, (tq, tk), 1)
            visible = k_pos <= q_pos
        f_q = f_ref[0]
        f_k = ft_ref[0]
        for hh in range(H_FOX):
            kt = kt_ref[0, hh * D_FOX:(hh + 1) * D_FOX, :]
            vt = vt_ref[0, hh * D_FOX:(hh + 1) * D_FOX, :]
            u = _dot(q_ref[0, hh], kt) - f_k[hh:hh + 1, :]
            if masked:
                u = jnp.where(visible, u, NEG_BIG)
            cq = f_q[:, hh:hh + 1]
            m_old = m_ref[hh]
            m_new = jnp.maximum(m_old, jnp.max(u, axis=-1, keepdims=True) + cq)
            alpha = jnp.exp(m_old - m_new)
            p = jnp.exp(u + (cq - m_new))
            l_ref[hh] = alpha * l_ref[hh] + jnp.sum(p, axis=-1, keepdims=True)
            acc_ref[hh] = alpha * acc_ref[hh] + _dot_nt(p.astype(BF16), vt)
            m_ref[hh] = m_new

    needs_mask = (ki + 1) * tk - 1 > qi * tq

    @pl.when(needs_mask)
    def _():
        step(True)

    @pl.when(jnp.logical_not(needs_mask))
    def _():
        step(False)

    @pl.when(ki == (qi * tq + tq - 1) // tk)
    def _():
        o_ref[0] = jnp.concatenate([acc_ref[hh] / l_ref[hh] for hh in range(H_FOX)], axis=-1).astype(BF16)


def _fox_prefill(fq, fk_tb, fv_tb, f, f_t):
    bsz, _, t, _ = fq.shape
    tq, tk = min(FOX_TQ, t), min(FOX_TK, t)
    qi, ki = _causal_pairs(t, tq, tk)
    c_fox = H_FOX * D_FOX
    grid_spec = pltpu.PrefetchScalarGridSpec(
        num_scalar_prefetch=2,
        grid=(bsz, qi.shape[0]),
        in_specs=[
            pl.BlockSpec((1, H_FOX, tq, D_FOX), lambda b, p, qi, ki: (b, 0, qi[p], 0)),
            pl.BlockSpec((1, c_fox, tk), lambda b, p, qi, ki: (b, 0, ki[p])),
            pl.BlockSpec((1, c_fox, tk), lambda b, p, qi, ki: (b, 0, ki[p])),
            pl.BlockSpec((1, tq, H_FOX), lambda b, p, qi, ki: (b, qi[p], 0)),
            pl.BlockSpec((1, H_FOX, tk), lambda b, p, qi, ki: (b, 0, ki[p])),
        ],
        out_specs=pl.BlockSpec((1, tq, c_fox), lambda b, p, qi, ki: (b, qi[p], 0)),
        scratch_shapes=[pltpu.VMEM((H_FOX, tq, 1), F32), pltpu.VMEM((H_FOX, tq, 1), F32),
                        pltpu.VMEM((H_FOX, tq, D_FOX), F32)],
    )
    return pl.pallas_call(
        functools.partial(_fox_prefill_kernel, tq=tq, tk=tk),
        grid_spec=grid_spec,
        out_shape=jax.ShapeDtypeStruct((bsz, t, c_fox), BF16),
        compiler_params=_params("parallel", "arbitrary"),
        name="fox_prefill",
    )(qi, ki, fq, fk_tb, fv_tb, f, f_t)


def _suffix_sums(x):
    n = x.shape[1]
    lane = lax.broadcasted_iota(jnp.int32, x.shape, 1)
    y = x
    k = 1
    while k < n:
        y = y + jnp.where(lane + k < n, pltpu.roll(y, n - k, 1), 0.0)
        k *= 2
    return y


def _decode_kernel(*refs, n_pages_step, n_new):
    pt_ref = refs[0]
    del pt_ref
    qlat_ref, qrope_ref, qbd_ref, cnew_ref, krnew_ref, fknew_ref, fvnew_ref, lfnew_ref, wuv_ref = refs[1:10]
    np_ = n_pages_step
    lat_refs = refs[10:10 + np_]
    krt_refs = refs[10 + np_:10 + 2 * np_]
    fkt_refs = refs[10 + 2 * np_:10 + 3 * np_]
    fvt_refs = refs[10 + 3 * np_:10 + 4 * np_]
    lft_refs = refs[10 + 4 * np_:10 + 5 * np_]
    omla_ref, ofox_ref = refs[10 + 5 * np_:12 + 5 * np_]
    (m1_ref, l1_ref, acc1_ref, m2_ref, l2_ref, acc2_ref, gq_ref, carry_ref) = refs[12 + 5 * np_:]

    c_id = pl.program_id(1)
    rows = qlat_ref.shape[1]
    q_lat = qlat_ref[0]
    q_rope = qrope_ref[0]
    q_bd = qbd_ref[0]

    def online_update(m_ref, l_ref, acc_ref, s, value_fn):
        m_old = m_ref[...]
        m_new = jnp.maximum(m_old, jnp.max(s, axis=-1, keepdims=True))
        alpha = jnp.exp(m_old - m_new)
        p = jnp.exp(s - m_new)
        l_ref[...] = alpha * l_ref[...] + jnp.sum(p, axis=-1, keepdims=True)
        acc_ref[...] = alpha * acc_ref[...] + value_fn(p.astype(BF16))
        m_ref[...] = m_new

    def add_forget_bias(s, g_keys):
        gq = gq_ref[...]
        return jnp.concatenate(
            [s[t * H_FOX:(t + 1) * H_FOX] + (g_keys - gq[:, t:t + 1]) for t in range(n_new)], axis=0)

    @pl.when(c_id == 0)
    def _():
        lf_new = lfnew_ref[0]
        incl = _suffix_sums(lf_new)
        g_new = incl - lf_new
        gq_ref[...] = g_new
        carry_ref[...] = jnp.broadcast_to(incl[:, 0:1], carry_ref.shape)
        row_tok = lax.broadcasted_iota(jnp.int32, (rows, NEW_PAD), 0) // H_FOX
        key = lax.broadcasted_iota(jnp.int32, (rows, NEW_PAD), 1)
        visible = key <= row_tok
        for ref in (m1_ref, m2_ref):
            ref[...] = jnp.full(ref.shape, NEG_BIG, F32)
        for ref in (l1_ref, l2_ref, acc1_ref, acc2_ref):
            ref[...] = jnp.zeros(ref.shape, F32)
        c_new = cnew_ref[0].astype(BF16)
        s1 = _dot_nt(q_lat, c_new) + _dot_nt(q_rope, krnew_ref[0].astype(BF16))
        online_update(m1_ref, l1_ref, acc1_ref, jnp.where(visible, s1, NEG_BIG), lambda p: _dot(p, c_new))
        fv_new = fvnew_ref[0].astype(BF16)
        s2 = add_forget_bias(_dot_nt(q_bd, fknew_ref[0].astype(BF16)), g_new[:, :NEW_PAD])
        online_update(m2_ref, l2_ref, acc2_ref, jnp.where(visible, s2, NEG_BIG), lambda p: _dot(p, fv_new))

    lat = jnp.concatenate([r[0] for r in lat_refs], axis=0).astype(BF16)
    krt = jnp.concatenate([r[0] for r in krt_refs], axis=1).astype(BF16)
    s1 = _dot_nt(q_lat, lat) + _dot(q_rope, krt)
    online_update(m1_ref, l1_ref, acc1_ref, s1, lambda p: _dot(p, lat))

    lf = jnp.concatenate([r[0] for r in lft_refs], axis=1)
    incl = _suffix_sums(lf)
    carry = carry_ref[:, 0:1]
    g_keys = carry + (incl - lf)
    carry_ref[...] = jnp.broadcast_to(carry + incl[:, 0:1], carry_ref.shape)
    fkt = jnp.concatenate([r[0] for r in fkt_refs], axis=1).astype(BF16)
    fvt = jnp.concatenate([r[0] for r in fvt_refs], axis=1).astype(BF16)
    s2 = add_forget_bias(_dot(q_bd, fkt), g_keys)
    online_update(m2_ref, l2_ref, acc2_ref, s2, lambda p: _dot_nt(p, fvt))

    @pl.when(c_id == pl.num_programs(1) - 1)
    def _():
        width = H_MLA * D_V_MLA
        row_head = lax.broadcasted_iota(jnp.int32, (rows, width), 0) % H_MLA
        col_head = lax.broadcasted_iota(jnp.int32, (rows, width), 1) // D_V_MLA
        own = row_head == col_head
        o_lat = (acc1_ref[...] / l1_ref[...]).astype(BF16)
        o1 = jnp.where(own, _dot(o_lat, wuv_ref[...]), 0.0)
        o2 = jnp.where(own, acc2_ref[...] / l2_ref[...], 0.0)
        omla_ref[0] = jnp.sum(o1.reshape(n_new, H_MLA, width), axis=1)
        ofox_ref[0] = jnp.sum(o2.reshape(n_new, H_FOX, width), axis=1)


def _paged_decode(page_table, q_lat, q_rope, q_bd, c_new, kr_new, fk_new, fv_new, lf_new_t, w_uv_all,
                  lat_pages, krt_pages, fkt_pages, fvt_pages, lft_pages):
    bsz, rows, c = q_lat.shape
    n_new = rows // H_MLA
    n_pages = page_table.shape[1]
    page = lat_pages.shape[1]
    nps = min(PAGES_PER_STEP, n_pages)
    n_steps = n_pages // nps
    c_fox = H_FOX * D_FOX
    pt_flat = page_table.reshape(-1)

    per_b = lambda shape: pl.BlockSpec((1,) + shape, lambda b, s, pt: (b, 0, 0))

    def paged(shape, j):
        def index_map(b, s, pt):
            return (pt[b * n_pages + (n_steps - 1 - s) * nps + j], 0, 0)
        return pl.BlockSpec((1,) + shape, index_map)

    in_specs = [per_b((rows, c)), per_b((rows, D_ROPE)), per_b((rows, c_fox)),
                per_b((NEW_PAD, c)), per_b((NEW_PAD, D_ROPE)), per_b((NEW_PAD, c_fox)), per_b((NEW_PAD, c_fox)),
                per_b((H_FOX, LANES)), _resident(w_uv_all.shape)]
    ins = [q_lat, q_rope, q_bd, c_new, kr_new, fk_new, fv_new, lf_new_t, w_uv_all]
    for arr, shape in ((lat_pages, (page, c)), (krt_pages, (D_ROPE, page)), (fkt_pages, (c_fox, page)),
                       (fvt_pages, (c_fox, page)), (lft_pages, (H_FOX, page))):
        for j in range(nps):
            in_specs.append(paged(shape, j))
            ins.append(arr)
    grid_spec = pltpu.PrefetchScalarGridSpec(
        num_scalar_prefetch=1,
        grid=(bsz, n_steps),
        in_specs=in_specs,
        out_specs=[pl.BlockSpec((1, n_new, c_fox), lambda b, s, pt: (b, 0, 0))] * 2,
        scratch_shapes=[pltpu.VMEM((rows, 1), F32), pltpu.VMEM((rows, 1), F32), pltpu.VMEM((rows, c), F32),
                        pltpu.VMEM((rows, 1), F32), pltpu.VMEM((rows, 1), F32), pltpu.VMEM((rows, c_fox), F32),
                        pltpu.VMEM((H_FOX, LANES), F32), pltpu.VMEM((H_FOX, LANES), F32)],
    )
    return pl.pallas_call(
        functools.partial(_decode_kernel, n_pages_step=nps, n_new=n_new),
        grid_spec=grid_spec,
        out_shape=[jax.ShapeDtypeStruct((bsz, n_new, c_fox), F32)] * 2,
        compiler_params=_params("parallel", "arbitrary"),
        name="paged_decode",
    )(pt_flat, *ins)


def _mix_out_kernel(omla_ref, ofox_ref, y_ref, wo_ref, g_ref, o_ref):
    half = omla_ref.shape[1]
    m = _dot(omla_ref[...].astype(BF16), wo_ref[:half, :]) + _dot(ofox_ref[...].astype(BF16), wo_ref[half:, :])
    o_ref[...] = y_ref[...] + _rms(m, g_ref[...])


def _mix_out(o_mla, o_fox, y, w_o, g_post):
    n, d = y.shape
    half = o_mla.shape[1]
    tm = min(ROW_TILE, n)
    return pl.pallas_call(
        _mix_out_kernel,
        grid=(n // tm,),
        in_specs=[pl.BlockSpec((tm, half), lambda i: (i, 0)), pl.BlockSpec((tm, half), lambda i: (i, 0)),
                  pl.BlockSpec((tm, d), lambda i: (i, 0)), _resident(w_o.shape), _resident((1, d))],
        out_specs=pl.BlockSpec((tm, d), lambda i: (i, 0)),
        out_shape=jax.ShapeDtypeStruct((n, d), F32),
        compiler_params=_params("parallel"),
        name="mix_out",
    )(o_mla, o_fox, y, w_o, g_post)


def _rope_tables(pos):
    freq = ROPE_THETA ** (-jnp.arange(HALF_ROPE, dtype=F32) * 2.0 / D_ROPE)
    ang = pos[:, None] * freq[None, :]
    cos, sin = jnp.cos(ang), jnp.sin(ang)
    cos_q = jnp.tile(jnp.concatenate([cos, cos], axis=1), (1, H_MLA))[None]
    sin_q = jnp.tile(jnp.concatenate([-sin, sin], axis=1), (1, H_MLA))[None]
    return {"cos_q": cos_q, "sin_q": sin_q, "cos_t": cos.T, "sin_t": sin.T}


def _prepare_weights(layer, w):
    g = lambda name: w[name][layer]
    q_lora = g("g_q_a").shape[0]
    kv_lora = g("g_kv_a").shape[0]
    c_fox = H_FOX * D_FOX
    w_in = g("w_in")
    o1 = q_lora
    o2 = o1 + kv_lora
    o3 = o2 + D_ROPE
    o4, o5, o6 = o3 + c_fox, o3 + 2 * c_fox, o3 + 3 * c_fox
    qa, kva, kr = w_in[:, :o1], w_in[:, o1:o2], w_in[:, o2:o3]
    fq, fk, fv, fl = w_in[:, o3:o4], w_in[:, o4:o5], w_in[:, o5:o6], w_in[:, o6:]
    pad = jnp.zeros((w_in.shape[0], LANES - D_ROPE - H_FOX), w_in.dtype)
    w_q_b = g("w_q_b").reshape(q_lora, H_MLA, D_NOPE + D_ROPE)
    w_kv_b = g("w_kv_b").reshape(kv_lora, H_MLA, D_NOPE + D_V_MLA)
    prm = {
        "w_in_n": jnp.concatenate([qa, kva, fq, fk, fv, kr, fl, pad], axis=1).astype(BF16),
        "w_in_n_t": jnp.concatenate([qa, kva, fq, fl], axis=1).astype(BF16),
        "w_in_t": jnp.concatenate([fk, fv, kr, fl], axis=1).T.astype(BF16),
        "w_q": jnp.concatenate([w_q_b[..., :D_NOPE].reshape(q_lora, -1),
                                w_q_b[..., D_NOPE:].reshape(q_lora, -1)], axis=1).astype(BF16),
        "w_uk_t": jnp.transpose(w_kv_b[..., :D_NOPE], (1, 2, 0)).astype(BF16),
        "w_uv": jnp.transpose(w_kv_b[..., D_NOPE:], (1, 0, 2)).astype(BF16),
        "w_uv_all": w_kv_b[..., D_NOPE:].reshape(kv_lora, -1).astype(BF16),
        "w_o": g("w_o").astype(BF16),
        "b_forget": g("b_forget")[None, :],
        "b_forget_t": g("b_forget")[:, None],
    }
    for name in ("g_pre_mix", "g_q_a", "g_kv_a", "g_post_mix", "g_pre_ffn1", "g_post_ffn1", "g_pre_ffn2", "g_post_ffn2"):
        prm[name] = g(name)[None, :]
    for tag in ("ffn1", "ffn2"):
        prm["w_gu_" + tag] = g("w_gu_" + tag).astype(BF16)
        prm["w_down_" + tag] = g("w_down_" + tag).astype(BF16)
    return prm


def _ffn(x, prm, tag):
    return _ffn_half(x, prm["g_pre_" + tag], prm["w_gu_" + tag], prm["w_down_" + tag], prm["g_post_" + tag])


def _prompt_layer(x, prm, rope):
    bsz, t, d = x.shape
    y1 = _ffn(x.reshape(bsz * t, d), prm, "ffn1")
    mi = _mix_in(y1.reshape(bsz, t, d), prm, rope, transposed=True)
    f, f_t = _fox_cumsum(mi["lf"], mi["lf_t"])
    o_mla = _mla_prefill(mi["q_lat"], mi["q_rope"], mi["ckv_b"], mi["kr_tb"], prm["w_uv"])
    o_fox = _fox_prefill(mi["fq"], mi["fk_tb"], mi["fv_tb"], f, f_t)
    y2 = _mix_out(o_mla.reshape(bsz * t, -1), o_fox.reshape(bsz * t, -1), y1, prm["w_o"], prm["g_post_mix"])
    out = _ffn(y2, prm, "ffn2").reshape(bsz, t, d)
    rows = (mi["ckv"],
            jnp.transpose(mi["kr_t"], (0, 2, 1)),
            jnp.transpose(mi["fk_t"].reshape(bsz, H_FOX, D_FOX, t), (0, 3, 1, 2)),
            jnp.transpose(mi["fv_t"].reshape(bsz, H_FOX, D_FOX, t), (0, 3, 1, 2)),
            jnp.transpose(mi["lf_t"], (0, 2, 1)))
    return out, rows


def _sample_layer(x, prm, rope, caches, page_table):
    bsz, n_new, d = x.shape
    n_tok = bsz * n_new
    c_fox = H_FOX * D_FOX
    y1 = _ffn(x.reshape(n_tok, d), prm, "ffn1")
    mi = _mix_in(y1.reshape(1, n_tok, d), prm, rope, transposed=False)

    def per_batch(a):
        return jnp.transpose(a[0].reshape(a.shape[1], bsz, n_new, a.shape[-1]), (1, 2, 0, 3)).reshape(
            bsz, n_new * a.shape[1], a.shape[-1])

    q_lat, q_rope = per_batch(mi["q_lat"]), per_batch(mi["q_rope"])
    fq = per_batch(mi["fq"]).reshape(bsz, n_new, H_FOX, 1, D_FOX)
    eye = jnp.eye(H_FOX, dtype=fq.dtype)[None, None, :, :, None]
    q_bd = (fq * eye).reshape(bsz, n_new * H_FOX, c_fox)

    def new_rows(a):
        a = a.reshape(bsz, n_new, a.shape[-1])
        return jnp.pad(a, ((0, 0), (0, NEW_PAD - n_new), (0, 0)))

    lf_new_t = jnp.pad(jnp.transpose(mi["lf"].reshape(bsz, n_new, H_FOX), (0, 2, 1)),
                       ((0, 0), (0, 0), (0, LANES - n_new)))
    lat, kr, fk, fv, lf = caches
    n_pool, page = lat.shape[0], lat.shape[1]
    krt_pages = jnp.transpose(kr, (0, 2, 1))
    fkt_pages = jnp.transpose(fk, (0, 2, 3, 1)).reshape(n_pool, c_fox, page)
    fvt_pages = jnp.transpose(fv, (0, 2, 3, 1)).reshape(n_pool, c_fox, page)
    lft_pages = jnp.transpose(lf, (0, 2, 1))
    o_mla, o_fox = _paged_decode(page_table, q_lat, q_rope, q_bd, new_rows(mi["ckv"]), new_rows(mi["kr"]),
                                 new_rows(mi["fk"]), new_rows(mi["fv"]), lf_new_t, prm["w_uv_all"],
                                 lat, krt_pages, fkt_pages, fvt_pages, lft_pages)
    y2 = _mix_out(o_mla.reshape(n_tok, -1), o_fox.reshape(n_tok, -1), y1, prm["w_o"], prm["g_post_mix"])
    out = _ffn(y2, prm, "ffn2").reshape(bsz, n_new, d)
    rows = (mi["ckv"].reshape(bsz, n_new, -1), mi["kr"].reshape(bsz, n_new, -1),
            mi["fk"].reshape(bsz, n_new, H_FOX, D_FOX), mi["fv"].reshape(bsz, n_new, H_FOX, D_FOX),
            mi["lf"].reshape(bsz, n_new, H_FOX))
    return out, rows


def kernel(x_prompt, x_sample, cache_mla_latent, cache_mla_krope, cache_fox_k, cache_fox_v, cache_fox_logf, page_table, g_pre_ffn1, w_gu_ffn1, w_down_ffn1, g_post_ffn1, g_pre_mix, w_in, g_q_a, w_q_b, g_kv_a, w_kv_b, b_forget, w_o, g_post_mix, g_pre_ffn2, w_gu_ffn2, w_down_ffn2, g_post_ffn2):
    weights = dict(g_pre_ffn1=g_pre_ffn1, w_gu_ffn1=w_gu_ffn1, w_down_ffn1=w_down_ffn1, g_post_ffn1=g_post_ffn1,
                   g_pre_mix=g_pre_mix, w_in=w_in, g_q_a=g_q_a, w_q_b=w_q_b, g_kv_a=g_kv_a, w_kv_b=w_kv_b,
                   b_forget=b_forget, w_o=w_o, g_post_mix=g_post_mix, g_pre_ffn2=g_pre_ffn2, w_gu_ffn2=w_gu_ffn2,
                   w_down_ffn2=w_down_ffn2, g_post_ffn2=g_post_ffn2)
    depth = w_in.shape[0]
    seq = x_prompt.shape[1]
    n_new = x_sample.shape[1]
    past_len = page_table.shape[1] * cache_mla_latent.shape[2]
    rope_p = _rope_tables(jnp.arange(seq, dtype=F32))
    rope_s = _rope_tables(jnp.tile(past_len + jnp.arange(n_new, dtype=F32), x_sample.shape[0]))
    yp, ys = x_prompt, x_sample
    p_rows = [[] for _ in range(5)]
    s_rows = [[] for _ in range(5)]
    for layer in range(depth):
        prm = _prepare_weights(layer, weights)
        yp, rows_p = _prompt_layer(yp, prm, rope_p)
        caches = (cache_mla_latent[layer], cache_mla_krope[layer], cache_fox_k[layer], cache_fox_v[layer],
                  cache_fox_logf[layer])
        ys, rows_s = _sample_layer(ys, prm, rope_s, caches, page_table)
        for i in range(5):
            p_rows[i].append(rows_p[i])
            s_rows[i].append(rows_s[i])
    return (yp, ys) + tuple(jnp.stack(r, axis=0) for r in p_rows) + tuple(jnp.stack(r, axis=0) for r in s_rows)
```

```python
import functools

import jax
import jax.numpy as jnp
import numpy as np
from jax import lax
from jax.experimental import pallas as pl
from jax.experimental.pallas import tpu as pltpu

H_MLA = 8
D_NOPE = 64
D_ROPE = 32
D_V_MLA = 64
H_FOX = 8
D_FOX = 64
ROPE_THETA = 10000.0
EPS = 1e-6
MLA_SCALE = (D_NOPE + D_ROPE) ** -0.5
FOX_SCALE = D_FOX ** -0.5
HALF_ROPE = D_ROPE // 2

F32 = jnp.float32
BF16 = jnp.bfloat16
NEG_BIG = -1e30

VMEM_LIMIT_BYTES = 56 * 1024 * 1024
LANES = 128

ROW_TILE = 512
FF_CHUNK = 256
MLA_TQ = 512
MLA_TK = 512
FOX_TQ = 512
FOX_TK = 512
SOFTMAX_ROWS = 64
CUMSUM_BLOCK = 256
PAGES_PER_STEP = 16
NEW_PAD = 16
FOX_V_ROWS = D_FOX + 16
LOG2E = 1.4426950408889634


def _dot(a, b):
    return jnp.dot(a, b, preferred_element_type=F32)


def _dot_nt(a, b):
    return lax.dot_general(a, b, (((1,), (1,)), ((), ())), preferred_element_type=F32)


def _rms(x, g):
    return x * lax.rsqrt(jnp.mean(x * x, axis=-1, keepdims=True) + EPS) * g


def _resident(shape):
    nd = len(shape)
    return pl.BlockSpec(shape, lambda *_: (0,) * nd, pipeline_mode=pl.Buffered(1))


def _params(*sem):
    return pltpu.CompilerParams(dimension_semantics=sem, vmem_limit_bytes=VMEM_LIMIT_BYTES)


def _ffn_kernel(x_ref, gpre_ref, wgu_ref, wd_ref, gpost_ref, o_ref, acc_ref, *, d_ff):
    x = x_ref[...]
    h = _rms(x, gpre_ref[...]).astype(BF16)
    for c in range(d_ff // FF_CHUNK):
        lo = c * FF_CHUNK
        a = _dot(h, wgu_ref[:, lo:lo + FF_CHUNK])
        b = _dot(h, wgu_ref[:, d_ff + lo:d_ff + lo + FF_CHUNK])
        act = (a * jax.nn.sigmoid(a) * b).astype(BF16)
        part = _dot(act, wd_ref[lo:lo + FF_CHUNK, :])
        if c == 0:
            acc_ref[...] = part
        else:
            acc_ref[...] += part
    o_ref[...] = x + 0.5 * _rms(acc_ref[...], gpost_ref[...])


def _ffn_half(x, g_pre, w_gu, w_down, g_post):
    n, d = x.shape
    d_ff = w_down.shape[0]
    tm = min(ROW_TILE, n)
    return pl.pallas_call(
        functools.partial(_ffn_kernel, d_ff=d_ff),
        grid=(n // tm,),
        in_specs=[
            pl.BlockSpec((tm, d), lambda i: (i, 0)),
            _resident((1, d)),
            _resident((d, 2 * d_ff)),
            _resident((d_ff, d)),
            _resident((1, d)),
        ],
        out_specs=pl.BlockSpec((tm, d), lambda i: (i, 0)),
        out_shape=jax.ShapeDtypeStruct((n, d), F32),
        scratch_shapes=[pltpu.VMEM((tm, d), F32)],
        compiler_params=_params("parallel"),
        name="ffn_half",
    )(x, g_pre, w_gu, w_down, g_post)


def _log_sigmoid(x):
    return -(jnp.maximum(-x, 0.0) + jnp.log1p(jnp.exp(-jnp.abs(x))))


def _mix_in_kernel(*refs, q_lora, kv_lora, transposed):
    if transposed:
        (y_ref, gpre_ref, wn_ref, wt_ref, gq_ref, wq_ref, gkv_ref, wuk_ref, b_ref, bt_ref,
         cosq_ref, sinq_ref, cost_ref, sint_ref,
         ckv_ref, q_ref, fq_ref, lf_ref, kfull_ref,
         fkt_ref, fktb_ref, fvt_ref, fvtb_ref, krt_ref, lft_ref) = refs
    else:
        (y_ref, gpre_ref, wn_ref, gq_ref, wq_ref, gkv_ref, wuk_ref, b_ref,
         cosq_ref, sinq_ref,
         ckv_ref, q_ref, fq_ref, lf_ref, fk_ref, fv_ref, kr_ref) = refs
    c_fox = H_FOX * D_FOX
    o_kv = q_lora
    o_fq = o_kv + kv_lora
    o_tail = o_fq + c_fox
    mla_scale = MLA_SCALE * LOG2E if transposed else MLA_SCALE
    fox_scale = FOX_SCALE * LOG2E if transposed else FOX_SCALE

    h = _rms(y_ref[0], gpre_ref[...]).astype(BF16)
    pn = _dot(h, wn_ref[...])

    ckv = _rms(pn[:, o_kv:o_fq], gkv_ref[...])
    ckv_ref[0] = ckv

    tail = pn[:, o_tail:o_tail + LANES]
    lane = lax.broadcasted_iota(jnp.int32, tail.shape, 1)
    partner = jnp.where(lane < HALF_ROPE, pltpu.roll(tail, LANES - HALF_ROPE, 1), pltpu.roll(tail, HALF_ROPE, 1))
    kr = (tail * cosq_ref[0][:, :LANES] + partner * sinq_ref[0][:, :LANES])[:, :D_ROPE]
    lf_ref[0] = _log_sigmoid(tail[:, D_ROPE:D_ROPE + H_FOX] + b_ref[...])

    qn = _rms(pn[:, :q_lora], gq_ref[...]).astype(BF16)
    q = _dot(qn, wq_ref[...])
    n_nope = H_MLA * D_NOPE
    qr = q[:, n_nope:]
    width = H_MLA * D_ROPE
    lane = lax.broadcasted_iota(jnp.int32, qr.shape, 1)
    partner = jnp.where((lane % D_ROPE) < HALF_ROPE,
                        pltpu.roll(qr, width - HALF_ROPE, 1), pltpu.roll(qr, HALF_ROPE, 1))
    qr = (qr * cosq_ref[0] + partner * sinq_ref[0]) * mla_scale
    for hh in range(H_MLA):
        qh = q[:, hh * D_NOPE:(hh + 1) * D_NOPE].astype(BF16)
        q_lat = _dot(qh, wuk_ref[hh]) * mla_scale
        q_ref[0, hh] = jnp.concatenate([q_lat, qr[:, hh * D_ROPE:(hh + 1) * D_ROPE]], axis=1).astype(BF16)

    fq = pn[:, o_fq:o_tail] * fox_scale
    for hh in range(H_FOX):
        fq_ref[0, hh] = fq[:, hh * D_FOX:(hh + 1) * D_FOX].astype(BF16)

    if transposed:
        kfull_ref[0] = jnp.concatenate([ckv, kr], axis=1).astype(BF16)
        pt = _dot_nt(wt_ref[...], h)
        fkt = pt[:c_fox]
        fvt = pt[c_fox:2 * c_fox]
        fkt_ref[0] = fkt
        fktb_ref[0] = fkt.astype(BF16)
        fvt_ref[0] = fvt
        tm = fvt.shape[1]
        ones_row = (lax.broadcasted_iota(jnp.int32, (FOX_V_ROWS - D_FOX, tm), 0) == 0).astype(F32)
        for hh in range(H_FOX):
            fvtb_ref[0, hh] = jnp.concatenate([fvt[hh * D_FOX:(hh + 1) * D_FOX], ones_row], axis=0).astype(BF16)
        o_kr = 2 * c_fox
        x1 = pt[o_kr:o_kr + HALF_ROPE]
        x2 = pt[o_kr + HALF_ROPE:o_kr + D_ROPE]
        cos, sin = cost_ref[...], sint_ref[...]
        krt_ref[0] = jnp.concatenate([x1 * cos - x2 * sin, x2 * cos + x1 * sin], axis=0)
        lft_ref[0] = _log_sigmoid(pt[o_kr + D_ROPE:o_kr + D_ROPE + H_FOX] + bt_ref[...])
    else:
        o_fk = o_tail + LANES
        fk_ref[0] = pn[:, o_fk:o_fk + c_fox]
        fv_ref[0] = pn[:, o_fk + c_fox:o_fk + 2 * c_fox]
        kr_ref[0] = kr


def _mix_in(y, prm, rope, *, transposed):
    bsz, t, d = y.shape
    tm = min(ROW_TILE, t)
    q_lora = prm["g_q_a"].shape[1]
    kv_lora = prm["g_kv_a"].shape[1]
    c_fox = H_FOX * D_FOX
    grid = (bsz, t // tm)
    tok = lambda c: pl.BlockSpec((1, tm, c), lambda b, i: (b, i, 0))
    head = lambda c: pl.BlockSpec((1, H_MLA, tm, c), lambda b, i: (b, 0, i, 0))
    chan = lambda c: pl.BlockSpec((1, c, tm), lambda b, i: (b, 0, i))
    sds = jax.ShapeDtypeStruct
    wn = prm["w_in_n_t"] if transposed else prm["w_in_n"]
    ins = [y, prm["g_pre_mix"], wn]
    in_specs = [tok(d), _resident((1, d)), _resident(wn.shape)]
    if transposed:
        ins.append(prm["w_in_t"])
        in_specs.append(_resident(prm["w_in_t"].shape))
    ins += [prm["g_q_a"], prm["w_q"], prm["g_kv_a"], prm["w_uk_t"], prm["b_forget"]]
    in_specs += [_resident((1, q_lora)), _resident(prm["w_q"].shape), _resident((1, kv_lora)),
                 _resident(prm["w_uk_t"].shape), _resident((1, H_FOX))]
    if transposed:
        ins.append(prm["b_forget_t"])
        in_specs.append(_resident((H_FOX, 1)))
    rw = H_MLA * D_ROPE
    ins += [rope["cos_q"], rope["sin_q"]]
    in_specs += [pl.BlockSpec((1, tm, rw), lambda b, i: (0, i, 0))] * 2
    if transposed:
        ins += [rope["cos_t"], rope["sin_t"]]
        in_specs += [pl.BlockSpec((HALF_ROPE, tm), lambda b, i: (0, i))] * 2
    qk_dim = kv_lora + D_ROPE
    names = ["ckv", "q", "fq", "lf"]
    out_shape = [sds((bsz, t, kv_lora), F32), sds((bsz, H_MLA, t, qk_dim), BF16),
                 sds((bsz, H_FOX, t, D_FOX), BF16), sds((bsz, t, H_FOX), F32)]
    out_specs = [tok(kv_lora), head(qk_dim), head(D_FOX), tok(H_FOX)]
    if transposed:
        names += ["k_full", "fk_t", "fk_tb", "fv_t", "fv_tb", "kr_t", "lf_t"]
        out_shape += [sds((bsz, t, qk_dim), BF16), sds((bsz, c_fox, t), F32), sds((bsz, c_fox, t), BF16),
                      sds((bsz, c_fox, t), F32), sds((bsz, H_FOX, FOX_V_ROWS, t), BF16),
                      sds((bsz, D_ROPE, t), F32), sds((bsz, H_FOX, t), F32)]
        out_specs += [tok(qk_dim)] + [chan(c_fox)] * 3 + [
            pl.BlockSpec((1, H_FOX, FOX_V_ROWS, tm), lambda b, i: (b, 0, 0, i)), chan(D_ROPE), chan(H_FOX)]
    else:
        names += ["fk", "fv", "kr"]
        out_shape += [sds((bsz, t, c_fox), F32), sds((bsz, t, c_fox), F32), sds((bsz, t, D_ROPE), F32)]
        out_specs += [tok(c_fox), tok(c_fox), tok(D_ROPE)]
    outs = pl.pallas_call(
        functools.partial(_mix_in_kernel, q_lora=q_lora, kv_lora=kv_lora, transposed=transposed),
        grid=grid, in_specs=in_specs, out_specs=out_specs, out_shape=out_shape,
        compiler_params=_params("parallel", "parallel"),
        name="mix_in_prompt" if transposed else "mix_in_sample",
    )(*ins)
    return dict(zip(names, outs))


def _cumsum_kernel(lf_ref, lft_ref, f_ref, ft_ref):
    t = lf_ref.shape[1]
    nb = t // CUMSUM_BLOCK
    r = lax.broadcasted_iota(jnp.int32, (CUMSUM_BLOCK, CUMSUM_BLOCK), 0)
    c = lax.broadcasted_iota(jnp.int32, (CUMSUM_BLOCK, CUMSUM_BLOCK), 1)
    tri = (c <= r).astype(F32)
    carry = jnp.zeros((1, H_FOX), F32)
    carry_t = jnp.zeros((H_FOX, 1), F32)
    for i in range(nb):
        lo = i * CUMSUM_BLOCK
        blk = jnp.dot(tri, lf_ref[0, lo:lo + CUMSUM_BLOCK, :], preferred_element_type=F32,
                      precision=lax.Precision.HIGHEST) + carry
        f_ref[0, lo:lo + CUMSUM_BLOCK, :] = blk * LOG2E
        carry = blk[CUMSUM_BLOCK - 1:CUMSUM_BLOCK, :]
        blk_t = lax.dot_general(lft_ref[0, :, lo:lo + CUMSUM_BLOCK], tri, (((1,), (1,)), ((), ())),
                                preferred_element_type=F32, precision=lax.Precision.HIGHEST) + carry_t
        ft_ref[0, :, lo:lo + CUMSUM_BLOCK] = blk_t * LOG2E
        carry_t = blk_t[:, CUMSUM_BLOCK - 1:CUMSUM_BLOCK]


def _fox_cumsum(lf, lf_t):
    bsz, t, hh = lf.shape
    return pl.pallas_call(
        _cumsum_kernel,
        grid=(bsz,),
        in_specs=[pl.BlockSpec((1, t, hh), lambda b: (b, 0, 0)), pl.BlockSpec((1, hh, t), lambda b: (b, 0, 0))],
        out_specs=[pl.BlockSpec((1, t, hh), lambda b: (b, 0, 0)), pl.BlockSpec((1, hh, t), lambda b: (b, 0, 0))],
        out_shape=[jax.ShapeDtypeStruct(lf.shape, F32), jax.ShapeDtypeStruct(lf_t.shape, F32)],
        compiler_params=_params("parallel"),
        name="fox_cumsum",
    )(lf, lf_t)


def _causal_pairs(t, tq, tk):
    qi, ki = [], []
    for i in range(t // tq):
        last = (i * tq + tq - 1) // tk
        for j in range(last + 1):
            qi.append(i)
            ki.append(j)
    return jnp.asarray(qi, jnp.int32), jnp.asarray(ki, jnp.int32)


def _causal_visible(qi, ki, tq, tk):
    q_pos = qi * tq + lax.broadcasted_iota(jnp.int32, (tq, tk), 0)
    k_pos = ki * tk + lax.broadcasted_iota(jnp.int32, (tq, tk), 1)
    return k_pos <= q_pos


def _row_max(s):
    mx = s[:, :LANES]
    for lo in range(LANES, s.shape[1], LANES):
        mx = jnp.maximum(mx, s[:, lo:lo + LANES])
    return jnp.max(mx, axis=-1, keepdims=True)


def _lane_tile(x, width):
    if width <= LANES:
        return x[:, :width]
    return jnp.concatenate([x] * (width // LANES), axis=1)


def _mla_prefill_kernel(qi_ref, ki_ref, q_ref, k_ref, wuv_ref, o_ref,
                        m_ref, l_ref, alpha_ref, acc_ref, s_ref, p_ref, *, tq, tk, c):
    p_id = pl.program_id(1)
    qi = qi_ref[p_id]
    ki = ki_ref[p_id]

    @pl.when(ki == 0)
    def _():
        m_ref[...] = jnp.full(m_ref.shape, NEG_BIG, F32)
        l_ref[...] = jnp.zeros(l_ref.shape, F32)
        acc_ref[...] = jnp.zeros(acc_ref.shape, F32)

    def step(masked):
        visible = _causal_visible(qi, ki, tq, tk) if masked else None

        def logits(hh):
            s = _dot_nt(q_ref[0, hh], k_ref[0])
            if masked:
                s = jnp.where(visible, s, NEG_BIG)
            s_ref[hh] = s
            m_old = m_ref[hh]
            m_new = jnp.maximum(m_old, _row_max(s))
            alpha_ref[hh] = jnp.exp2(m_old - m_new)
            m_ref[hh] = m_new

        logits(0)
        for hh in range(H_MLA):
            if hh + 1 < H_MLA:
                logits(hh + 1)
            for r0 in range(0, tq, SOFTMAX_ROWS):
                rows = slice(r0, r0 + SOFTMAX_ROWS)
                m_new = m_ref[hh, rows]
                p_sum = None
                for lo in range(0, tk, LANES):
                    p = jnp.exp2(s_ref[hh, rows, lo:lo + LANES] - m_new)
                    p_ref[hh, rows, lo:lo + LANES] = p.astype(BF16)
                    p_sum = p if p_sum is None else p_sum + p
                l_ref[hh, rows] = alpha_ref[hh, rows] * l_ref[hh, rows] + p_sum
            acc_ref[hh] = _lane_tile(alpha_ref[hh], c) * acc_ref[hh] + _dot(p_ref[hh], k_ref[0, :, :c])

    needs_mask = (ki + 1) * tk - 1 > qi * tq

    @pl.when(needs_mask)
    def _():
        step(True)

    @pl.when(jnp.logical_not(needs_mask))
    def _():
        step(False)

    @pl.when(ki == (qi * tq + tq - 1) // tk)
    def _():
        outs = []
        for hh in range(H_MLA):
            o_lat = acc_ref[hh] / jnp.sum(l_ref[hh], axis=-1, keepdims=True)
            outs.append(_dot(o_lat.astype(BF16), wuv_ref[hh]))
        o_ref[0] = jnp.concatenate(outs, axis=-1).astype(BF16)


def _mla_prefill(q, k_full, w_uv):
    bsz, _, t, qk_dim = q.shape
    c = qk_dim - D_ROPE
    tq, tk = min(MLA_TQ, t), min(MLA_TK, t)
    qi, ki = _causal_pairs(t, tq, tk)
    grid_spec = pltpu.PrefetchScalarGridSpec(
        num_scalar_prefetch=2,
        grid=(bsz, qi.shape[0]),
        in_specs=[
            pl.BlockSpec((1, H_MLA, tq, qk_dim), lambda b, p, qi, ki: (b, 0, qi[p], 0)),
            pl.BlockSpec((1, tk, qk_dim), lambda b, p, qi, ki: (b, ki[p], 0)),
            _resident(w_uv.shape),
        ],
        out_specs=pl.BlockSpec((1, tq, H_MLA * D_V_MLA), lambda b, p, qi, ki: (b, qi[p], 0)),
        scratch_shapes=[pltpu.VMEM((H_MLA, tq, LANES), F32), pltpu.VMEM((H_MLA, tq, LANES), F32),
                        pltpu.VMEM((H_MLA, tq, LANES), F32), pltpu.VMEM((H_MLA, tq, c), F32),
                        pltpu.VMEM((H_MLA, tq, tk), F32), pltpu.VMEM((H_MLA, tq, tk), BF16)],
    )
    return pl.pallas_call(
        functools.partial(_mla_prefill_kernel, tq=tq, tk=tk, c=c),
        grid_spec=grid_spec,
        out_shape=jax.ShapeDtypeStruct((bsz, t, H_MLA * D_V_MLA), BF16),
        compiler_params=_params("parallel", "arbitrary"),
        name="mla_prefill",
    )(qi, ki, q, k_full, w_uv)


def _fox_prefill_kernel(qi_ref, ki_ref, q_ref, kt_ref, vt_ref, f_ref, ft_ref, o_ref,
                        m_ref, alpha_ref, cq_ref, shift_ref, acc_ref, s_ref, p_ref, *, tq, tk):
    p_id = pl.program_id(1)
    qi = qi_ref[p_id]
    ki = ki_ref[p_id]

    @pl.when(ki == 0)
    def _():
        m_ref[...] = jnp.full(m_ref.shape, NEG_BIG, F32)
        acc_ref[...] = jnp.zeros(acc_ref.shape, F32)
        f_q = f_ref[0]
        for hh in range(H_FOX):
            cq_ref[hh] = jnp.broadcast_to(f_q[:, hh:hh + 1], (tq, LANES))

    def step(masked):
        visible = _causal_visible(qi, ki, tq, tk) if masked else None

        def logits(hh):
            u = _dot(q_ref[0, hh], kt_ref[0, hh * D_FOX:(hh + 1) * D_FOX, :]) - ft_ref[0, hh:hh + 1, :]
            if masked:
                u = jnp.where(visible, u, NEG_BIG)
            s_ref[hh] = u
            cq = cq_ref[hh]
            m_old = m_ref[hh]
            m_new = jnp.maximum(m_old, _row_max(u) + cq)
            alpha_ref[hh] = jnp.exp2(m_old - m_new)
            m_ref[hh] = m_new
            shift_ref[hh] = cq - m_new

        logits(0)
        for hh in range(H_FOX):
            if hh + 1 < H_FOX:
                logits(hh + 1)
            for r0 in range(0, tq, SOFTMAX_ROWS):
                rows = slice(r0, r0 + SOFTMAX_ROWS)
                shift = shift_ref[hh, rows]
                for lo in range(0, tk, LANES):
                    p_ref[hh, rows, lo:lo + LANES] = jnp.exp2(s_ref[hh, rows, lo:lo + LANES] + shift).astype(BF16)
            acc_ref[hh] = (_lane_tile(alpha_ref[hh], FOX_V_ROWS) * acc_ref[hh]
                           + _dot_nt(p_ref[hh], vt_ref[0, hh]))

    needs_mask = (ki + 1) * tk - 1 > qi * tq

    @pl.when(needs_mask)
    def _():
        step(True)

    @pl.when(jnp.logical_not(needs_mask))
    def _():
        step(False)

    @pl.when(ki == (qi * tq + tq - 1) // tk)
    def _():
        outs = []
        for hh in range(H_FOX):
            acc = acc_ref[hh]
            outs.append(acc[:, :D_FOX] / acc[:, D_FOX:D_FOX + 1])
        o_ref[0] = jnp.concatenate(outs, axis=-1).astype(BF16)


def _fox_prefill(fq, fk_tb, fv_tb, f, f_t):
    bsz, _, t, _ = fq.shape
    tq, tk = min(FOX_TQ, t), min(FOX_TK, t)
    qi, ki = _causal_pairs(t, tq, tk)
    c_fox = H_FOX * D_FOX
    grid_spec = pltpu.PrefetchScalarGridSpec(
        num_scalar_prefetch=2,
        grid=(bsz, qi.shape[0]),
        in_specs=[
            pl.BlockSpec((1, H_FOX, tq, D_FOX), lambda b, p, qi, ki: (b, 0, qi[p], 0)),
            pl.BlockSpec((1, c_fox, tk), lambda b, p, qi, ki: (b, 0, ki[p])),
            pl.BlockSpec((1, H_FOX, FOX_V_ROWS, tk), lambda b, p, qi, ki: (b, 0, 0, ki[p])),
            pl.BlockSpec((1, tq, H_FOX), lambda b, p, qi, ki: (b, qi[p], 0)),
            pl.BlockSpec((1, H_FOX, tk), lambda b, p, qi, ki: (b, 0, ki[p])),
        ],
        out_specs=pl.BlockSpec((1, tq, c_fox), lambda b, p, qi, ki: (b, qi[p], 0)),
        scratch_shapes=[pltpu.VMEM((H_FOX, tq, LANES), F32), pltpu.VMEM((H_FOX, tq, LANES), F32),
                        pltpu.VMEM((H_FOX, tq, LANES), F32), pltpu.VMEM((H_FOX, tq, LANES), F32),
                        pltpu.VMEM((H_FOX, tq, FOX_V_ROWS), F32),
                        pltpu.VMEM((H_FOX, tq, tk), F32), pltpu.VMEM((H_FOX, tq, tk), BF16)],
    )
    return pl.pallas_call(
        functools.partial(_fox_prefill_kernel, tq=tq, tk=tk),
        grid_spec=grid_spec,
        out_shape=jax.ShapeDtypeStruct((bsz, t, c_fox), BF16),
        compiler_params=_params("parallel", "arbitrary"),
        name="fox_prefill",
    )(qi, ki, fq, fk_tb, fv_tb, f, f_t)


def _suffix_sums(x):
    n = x.shape[1]
    lane = lax.broadcasted_iota(jnp.int32, x.shape, 1)
    y = x
    k = 1
    while k < n:
        y = y + jnp.where(lane + k < n, pltpu.roll(y, n - k, 1), 0.0)
        k *= 2
    return y


def _decode_kernel(pt_ref, qlat_ref, qrope_ref, qbd_ref, cnew_ref, krnew_ref, fknew_ref, fvnew_ref, lfnew_ref,
                   wuv_ref, lat_hbm, krt_hbm, fkt_hbm, fvt_hbm, lft_hbm, omla_ref, ofox_ref,
                   m1_ref, l1_ref, acc1_ref, m2_ref, l2_ref, acc2_ref, gq_ref, carry_ref,
                   lat_buf, krt_buf, fkt_buf, fvt_buf, lft_buf, sems, *, nps, n_pages, n_new):
    b_id = pl.program_id(0)
    c_id = pl.program_id(1)
    n_steps = pl.num_programs(1)
    g_id = b_id * n_steps + c_id
    slot = g_id % 2
    pairs = ((lat_hbm, lat_buf), (krt_hbm, krt_buf), (fkt_hbm, fkt_buf), (fvt_hbm, fvt_buf), (lft_hbm, lft_buf))

    def page_copies(b, c, slot_):
        base = b * n_pages + (n_steps - 1 - c) * nps
        copies = []
        for j in range(nps):
            page = pt_ref[base + j]
            for k, (hbm, buf) in enumerate(pairs):
                copies.append(pltpu.make_async_copy(hbm.at[page], buf.at[slot_, j], sems.at[slot_, k]))
        return copies

    @pl.when(g_id == 0)
    def _():
        for cp in page_copies(b_id, c_id, slot):
            cp.start()

    @pl.when(g_id + 1 < pl.num_programs(0) * n_steps)
    def _():
        wrap = c_id + 1 == n_steps
        for cp in page_copies(jnp.where(wrap, b_id + 1, b_id), jnp.where(wrap, 0, c_id + 1), 1 - slot):
            cp.start()

    for cp in page_copies(b_id, c_id, slot):
        cp.wait()

    rows = qlat_ref.shape[1]
    q_lat = qlat_ref[0]
    q_rope = qrope_ref[0]
    q_bd = qbd_ref[0]

    def online_update(m_ref, l_ref, acc_ref, s, value_fn):
        m_old = m_ref[...]
        m_new = jnp.maximum(m_old, jnp.max(s, axis=-1, keepdims=True))
        alpha = jnp.exp(m_old - m_new)
        p = jnp.exp(s - m_new)
        l_ref[...] = alpha * l_ref[...] + jnp.sum(p, axis=-1, keepdims=True)
        acc_ref[...] = alpha * acc_ref[...] + value_fn(p.astype(BF16))
        m_ref[...] = m_new

    def add_forget_bias(s, g_keys):
        gq = gq_ref[...]
        return jnp.concatenate(
            [s[t * H_FOX:(t + 1) * H_FOX] + (g_keys - gq[:, t:t + 1]) for t in range(n_new)], axis=0)

    @pl.when(c_id == 0)
    def _():
        lf_new = lfnew_ref[0]
        incl = _suffix_sums(lf_new)
        g_new = incl - lf_new
        gq_ref[...] = g_new
        carry_ref[...] = jnp.broadcast_to(incl[:, 0:1], carry_ref.shape)
        row_tok = lax.broadcasted_iota(jnp.int32, (rows, NEW_PAD), 0) // H_FOX
        key = lax.broadcasted_iota(jnp.int32, (rows, NEW_PAD), 1)
        visible = key <= row_tok
        for ref in (m1_ref, m2_ref):
            ref[...] = jnp.full(ref.shape, NEG_BIG, F32)
        for ref in (l1_ref, l2_ref, acc1_ref, acc2_ref):
            ref[...] = jnp.zeros(ref.shape, F32)
        c_new = cnew_ref[0].astype(BF16)
        s1 = _dot_nt(q_lat, c_new) + _dot_nt(q_rope, krnew_ref[0].astype(BF16))
        online_update(m1_ref, l1_ref, acc1_ref, jnp.where(visible, s1, NEG_BIG), lambda p: _dot(p, c_new))
        fv_new = fvnew_ref[0].astype(BF16)
        s2 = add_forget_bias(_dot_nt(q_bd, fknew_ref[0].astype(BF16)), g_new[:, :NEW_PAD])
        online_update(m2_ref, l2_ref, acc2_ref, jnp.where(visible, s2, NEG_BIG), lambda p: _dot(p, fv_new))

    def along_lanes(buf):
        return jnp.concatenate([buf[slot, j] for j in range(nps)], axis=1)

    lat = jnp.concatenate([lat_buf[slot, j] for j in range(nps)], axis=0).astype(BF16)
    krt = along_lanes(krt_buf).astype(BF16)
    s1 = _dot_nt(q_lat, lat) + _dot(q_rope, krt)
    online_update(m1_ref, l1_ref, acc1_ref, s1, lambda p: _dot(p, lat))

    lf = along_lanes(lft_buf)
    incl = _suffix_sums(lf)
    carry = carry_ref[:, 0:1]
    g_keys = carry + (incl - lf)
    carry_ref[...] = jnp.broadcast_to(carry + incl[:, 0:1], carry_ref.shape)
    fkt = along_lanes(fkt_buf).astype(BF16)
    fvt = along_lanes(fvt_buf).astype(BF16)
    s2 = add_forget_bias(_dot(q_bd, fkt), g_keys)
    online_update(m2_ref, l2_ref, acc2_ref, s2, lambda p: _dot_nt(p, fvt))

    @pl.when(c_id == pl.num_programs(1) - 1)
    def _():
        width = H_MLA * D_V_MLA
        row_head = lax.broadcasted_iota(jnp.int32, (rows, width), 0) % H_MLA
        col_head = lax.broadcasted_iota(jnp.int32, (rows, width), 1) // D_V_MLA
        own = row_head == col_head
        o_lat = (acc1_ref[...] / l1_ref[...]).astype(BF16)
        o1 = jnp.where(own, _dot(o_lat, wuv_ref[...]), 0.0)
        o2 = jnp.where(own, acc2_ref[...] / l2_ref[...], 0.0)
        omla_ref[0] = jnp.sum(o1.reshape(n_new, H_MLA, width), axis=1)
        ofox_ref[0] = jnp.sum(o2.reshape(n_new, H_FOX, width), axis=1)


def _paged_decode(page_table, q_lat, q_rope, q_bd, c_new, kr_new, fk_new, fv_new, lf_new_t, w_uv_all,
                  lat_pages, krt_pages, fkt_pages, fvt_pages, lft_pages):
    bsz, rows, c = q_lat.shape
    n_new = rows // H_MLA
    n_pages = page_table.shape[1]
    page = lat_pages.shape[1]
    nps = min(PAGES_PER_STEP, n_pages)
    n_steps = n_pages // nps
    c_fox = H_FOX * D_FOX
    pt_flat = page_table.reshape(-1)

    per_b = lambda shape: pl.BlockSpec((1,) + shape, lambda b, s, pt: (b, 0, 0))
    in_specs = [per_b((rows, c)), per_b((rows, D_ROPE)), per_b((rows, c_fox)),
                per_b((NEW_PAD, c)), per_b((NEW_PAD, D_ROPE)), per_b((NEW_PAD, c_fox)), per_b((NEW_PAD, c_fox)),
                per_b((H_FOX, LANES)), _resident(w_uv_all.shape)] + [pl.BlockSpec(memory_space=pl.ANY)] * 5
    ins = [q_lat, q_rope, q_bd, c_new, kr_new, fk_new, fv_new, lf_new_t, w_uv_all,
           lat_pages, krt_pages, fkt_pages, fvt_pages, lft_pages]
    page_bufs = [pltpu.VMEM((2, nps) + shape, F32)
                 for shape in ((page, c), (D_ROPE, page), (c_fox, page), (c_fox, page), (H_FOX, page))]
    grid_spec = pltpu.PrefetchScalarGridSpec(
        num_scalar_prefetch=1,
        grid=(bsz, n_steps),
        in_specs=in_specs,
        out_specs=[pl.BlockSpec((1, n_new, c_fox), lambda b, s, pt: (b, 0, 0))] * 2,
        scratch_shapes=[pltpu.VMEM((rows, 1), F32), pltpu.VMEM((rows, 1), F32), pltpu.VMEM((rows, c), F32),
                        pltpu.VMEM((rows, 1), F32), pltpu.VMEM((rows, 1), F32), pltpu.VMEM((rows, c_fox), F32),
                        pltpu.VMEM((H_FOX, LANES), F32), pltpu.VMEM((H_FOX, LANES), F32)]
        + page_bufs + [pltpu.SemaphoreType.DMA((2, len(page_bufs)))],
    )
    return pl.pallas_call(
        functools.partial(_decode_kernel, nps=nps, n_pages=n_pages, n_new=n_new),
        grid_spec=grid_spec,
        out_shape=[jax.ShapeDtypeStruct((bsz, n_new, c_fox), F32)] * 2,
        compiler_params=_params("arbitrary", "arbitrary"),
        name="paged_decode",
    )(pt_flat, *ins)


def _mix_out_kernel(omla_ref, ofox_ref, y_ref, wo_ref, g_ref, o_ref):
    half = omla_ref.shape[1]
    m = _dot(omla_ref[...].astype(BF16), wo_ref[:half, :]) + _dot(ofox_ref[...].astype(BF16), wo_ref[half:, :])
    o_ref[...] = y_ref[...] + _rms(m, g_ref[...])


def _mix_out(o_mla, o_fox, y, w_o, g_post):
    n, d = y.shape
    half = o_mla.shape[1]
    tm = min(ROW_TILE, n)
    return pl.pallas_call(
        _mix_out_kernel,
        grid=(n // tm,),
        in_specs=[pl.BlockSpec((tm, half), lambda i: (i, 0)), pl.BlockSpec((tm, half), lambda i: (i, 0)),
                  pl.BlockSpec((tm, d), lambda i: (i, 0)), _resident(w_o.shape), _resident((1, d))],
        out_specs=pl.BlockSpec((tm, d), lambda i: (i, 0)),
        out_shape=jax.ShapeDtypeStruct((n, d), F32),
        compiler_params=_params("parallel"),
        name="mix_out",
    )(o_mla, o_fox, y, w_o, g_post)


def _rope_tables(pos):
    freq = ROPE_THETA ** (-jnp.arange(HALF_ROPE, dtype=F32) * 2.0 / D_ROPE)
    ang = pos[:, None] * freq[None, :]
    cos, sin = jnp.cos(ang), jnp.sin(ang)
    cos_q = jnp.tile(jnp.concatenate([cos, cos], axis=1), (1, H_MLA))[None]
    sin_q = jnp.tile(jnp.concatenate([-sin, sin], axis=1), (1, H_MLA))[None]
    return {"cos_q": cos_q, "sin_q": sin_q, "cos_t": cos.T, "sin_t": sin.T}


def _prepare_weights(layer, w):
    g = lambda name: w[name][layer]
    q_lora = g("g_q_a").shape[0]
    kv_lora = g("g_kv_a").shape[0]
    c_fox = H_FOX * D_FOX
    w_in = g("w_in")
    o1 = q_lora
    o2 = o1 + kv_lora
    o3 = o2 + D_ROPE
    o4, o5, o6 = o3 + c_fox, o3 + 2 * c_fox, o3 + 3 * c_fox
    qa, kva, kr = w_in[:, :o1], w_in[:, o1:o2], w_in[:, o2:o3]
    fq, fk, fv, fl = w_in[:, o3:o4], w_in[:, o4:o5], w_in[:, o5:o6], w_in[:, o6:]
    pad = jnp.zeros((w_in.shape[0], LANES - D_ROPE - H_FOX), w_in.dtype)
    w_q_b = g("w_q_b").reshape(q_lora, H_MLA, D_NOPE + D_ROPE)
    w_kv_b = g("w_kv_b").reshape(kv_lora, H_MLA, D_NOPE + D_V_MLA)
    prm = {
        "w_in_n": jnp.concatenate([qa, kva, fq, kr, fl, pad, fk, fv], axis=1).astype(BF16),
        "w_in_n_t": jnp.concatenate([qa, kva, fq, kr, fl, pad], axis=1).astype(BF16),
        "w_in_t": jnp.concatenate([fk, fv, kr, fl], axis=1).T.astype(BF16),
        "w_q": jnp.concatenate([w_q_b[..., :D_NOPE].reshape(q_lora, -1),
                                w_q_b[..., D_NOPE:].reshape(q_lora, -1)], axis=1).astype(BF16),
        "w_uk_t": jnp.transpose(w_kv_b[..., :D_NOPE], (1, 2, 0)).astype(BF16),
        "w_uv": jnp.transpose(w_kv_b[..., D_NOPE:], (1, 0, 2)).astype(BF16),
        "w_uv_all": w_kv_b[..., D_NOPE:].reshape(kv_lora, -1).astype(BF16),
        "w_o": g("w_o").astype(BF16),
        "b_forget": g("b_forget")[None, :],
        "b_forget_t": g("b_forget")[:, None],
    }
    for name in ("g_pre_mix", "g_q_a", "g_kv_a", "g_post_mix", "g_pre_ffn1", "g_post_ffn1", "g_pre_ffn2", "g_post_ffn2"):
        prm[name] = g(name)[None, :]
    for tag in ("ffn1", "ffn2"):
        prm["w_gu_" + tag] = g("w_gu_" + tag).astype(BF16)
        prm["w_down_" + tag] = g("w_down_" + tag).astype(BF16)
    return prm


def _ffn(x, prm, tag):
    return _ffn_half(x, prm["g_pre_" + tag], prm["w_gu_" + tag], prm["w_down_" + tag], prm["g_post_" + tag])


def _prompt_layer(x, prm, rope):
    bsz, t, d = x.shape
    y1 = _ffn(x.reshape(bsz * t, d), prm, "ffn1")
    mi = _mix_in(y1.reshape(bsz, t, d), prm, rope, transposed=True)
    f, f_t = _fox_cumsum(mi["lf"], mi["lf_t"])
    o_mla = _mla_prefill(mi["q"], mi["k_full"], prm["w_uv"])
    o_fox = _fox_prefill(mi["fq"], mi["fk_tb"], mi["fv_tb"], f, f_t)
    y2 = _mix_out(o_mla.reshape(bsz * t, -1), o_fox.reshape(bsz * t, -1), y1, prm["w_o"], prm["g_post_mix"])
    out = _ffn(y2, prm, "ffn2").reshape(bsz, t, d)
    rows = (mi["ckv"],
            jnp.transpose(mi["kr_t"], (0, 2, 1)),
            jnp.transpose(mi["fk_t"].reshape(bsz, H_FOX, D_FOX, t), (0, 3, 1, 2)),
            jnp.transpose(mi["fv_t"].reshape(bsz, H_FOX, D_FOX, t), (0, 3, 1, 2)),
            jnp.transpose(mi["lf_t"], (0, 2, 1)))
    return out, rows


def _sample_layer(x, prm, rope, caches, page_table):
    bsz, n_new, d = x.shape
    n_tok = bsz * n_new
    c_fox = H_FOX * D_FOX
    y1 = _ffn(x.reshape(n_tok, d), prm, "ffn1")
    mi = _mix_in(y1.reshape(1, n_tok, d), prm, rope, transposed=False)

    def per_batch(a):
        return jnp.transpose(a[0].reshape(a.shape[1], bsz, n_new, a.shape[-1]), (1, 2, 0, 3)).reshape(
            bsz, n_new * a.shape[1], a.shape[-1])

    q = per_batch(mi["q"])
    kv_lora = q.shape[-1] - D_ROPE
    q_lat, q_rope = q[..., :kv_lora], q[..., kv_lora:]
    fq = per_batch(mi["fq"]).reshape(bsz, n_new, H_FOX, 1, D_FOX)
    eye = jnp.eye(H_FOX, dtype=fq.dtype)[None, None, :, :, None]
    q_bd = (fq * eye).reshape(bsz, n_new * H_FOX, c_fox)

    def new_rows(a):
        a = a.reshape(bsz, n_new, a.shape[-1])
        return jnp.pad(a, ((0, 0), (0, NEW_PAD - n_new), (0, 0)))

    lf_new_t = jnp.pad(jnp.transpose(mi["lf"].reshape(bsz, n_new, H_FOX), (0, 2, 1)),
                       ((0, 0), (0, 0), (0, LANES - n_new)))
    lat, kr, fk, fv, lf = caches
    n_pool, page = lat.shape[0], lat.shape[1]
    krt_pages = jnp.transpose(kr, (0, 2, 1))
    fkt_pages = jnp.transpose(fk, (0, 2, 3, 1)).reshape(n_pool, c_fox, page)
    fvt_pages = jnp.transpose(fv, (0, 2, 3, 1)).reshape(n_pool, c_fox, page)
    lft_pages = jnp.transpose(lf, (0, 2, 1))
    o_mla, o_fox = _paged_decode(page_table, q_lat, q_rope, q_bd, new_rows(mi["ckv"]), new_rows(mi["kr"]),
                                 new_rows(mi["fk"]), new_rows(mi["fv"]), lf_new_t, prm["w_uv_all"],
                                 lat, krt_pages, fkt_pages, fvt_pages, lft_pages)
    y2 = _mix_out(o_mla.reshape(n_tok, -1), o_fox.reshape(n_tok, -1), y1, prm["w_o"], prm["g_post_mix"])
    out = _ffn(y2, prm, "ffn2").reshape(bsz, n_new, d)
    rows = (mi["ckv"].reshape(bsz, n_new, -1), mi["kr"].reshape(bsz, n_new, -1),
            mi["fk"].reshape(bsz, n_new, H_FOX, D_FOX), mi["fv"].reshape(bsz, n_new, H_FOX, D_FOX),
            mi["lf"].reshape(bsz, n_new, H_FOX))
    return out, rows


def kernel(x_prompt, x_sample, cache_mla_latent, cache_mla_krope, cache_fox_k, cache_fox_v, cache_fox_logf, page_table, g_pre_ffn1, w_gu_ffn1, w_down_ffn1, g_post_ffn1, g_pre_mix, w_in, g_q_a, w_q_b, g_kv_a, w_kv_b, b_forget, w_o, g_post_mix, g_pre_ffn2, w_gu_ffn2, w_down_ffn2, g_post_ffn2):
    weights = dict(g_pre_ffn1=g_pre_ffn1, w_gu_ffn1=w_gu_ffn1, w_down_ffn1=w_down_ffn1, g_post_ffn1=g_post_ffn1,
                   g_pre_mix=g_pre_mix, w_in=w_in, g_q_a=g_q_a, w_q_b=w_q_b, g_kv_a=g_kv_a, w_kv_b=w_kv_b,
                   b_forget=b_forget, w_o=w_o, g_post_mix=g_post_mix, g_pre_ffn2=g_pre_ffn2, w_gu_ffn2=w_gu_ffn2,
                   w_down_ffn2=w_down_ffn2, g_post_ffn2=g_post_ffn2)
    depth = w_in.shape[0]
    seq = x_prompt.shape[1]
    n_new = x_sample.shape[1]
    past_len = page_table.shape[1] * cache_mla_latent.shape[2]
    rope_p = _rope_tables(jnp.arange(seq, dtype=F32))
    rope_s = _rope_tables(jnp.tile(past_len + jnp.arange(n_new, dtype=F32), x_sample.shape[0]))
    yp, ys = x_prompt, x_sample
    p_rows = [[] for _ in range(5)]
    s_rows = [[] for _ in range(5)]
    for layer in range(depth):
        prm = _prepare_weights(layer, weights)
        yp, rows_p = _prompt_layer(yp, prm, rope_p)
        caches = (cache_mla_latent[layer], cache_mla_krope[layer], cache_fox_k[layer], cache_fox_v[layer],
                  cache_fox_logf[layer])
        ys, rows_s = _sample_layer(ys, prm, rope_s, caches, page_table)
        for i in range(5):
            p_rows[i].append(rows_p[i])
            s_rows[i].append(rows_s[i])
    return (yp, ys) + tuple(jnp.stack(r, axis=0) for r in p_rows) + tuple(jnp.stack(r, axis=0) for r in s_rows)
```

```python
import functools

import jax
import jax.numpy as jnp
import numpy as np
from jax import lax
from jax.experimental import pallas as pl
from jax.experimental.pallas import tpu as pltpu

H_MLA = 8
D_NOPE = 64
D_ROPE = 32
D_V_MLA = 64
H_FOX = 8
D_FOX = 64
ROPE_THETA = 10000.0
EPS = 1e-6
MLA_SCALE = (D_NOPE + D_ROPE) ** -0.5
FOX_SCALE = D_FOX ** -0.5
HALF_ROPE = D_ROPE // 2

F32 = jnp.float32
BF16 = jnp.bfloat16
NEG_BIG = -1e30

VMEM_LIMIT_BYTES = 56 * 1024 * 1024
LANES = 128

FFN_ROW_TILE = 1024
ATTN_TILE = 512
FF_CHUNK = 256
SOFTMAX_ROWS = 64
CUMSUM_BLOCK = 256
PAGES_PER_STEP = 32
NEW_PAD = 16
FOX_V_ROWS = D_FOX + 16
LOG2E = 1.4426950408889634


def _dot(a, b):
    return jnp.dot(a, b, preferred_element_type=F32)


def _dot_nt(a, b):
    return lax.dot_general(a, b, (((1,), (1,)), ((), ())), preferred_element_type=F32)


def _rms(x, g):
    return x * lax.rsqrt(jnp.mean(x * x, axis=-1, keepdims=True) + EPS) * g


def _resident(shape):
    nd = len(shape)
    return pl.BlockSpec(shape, lambda *_: (0,) * nd, pipeline_mode=pl.Buffered(1))


def _params(*sem):
    return pltpu.CompilerParams(dimension_semantics=sem, vmem_limit_bytes=VMEM_LIMIT_BYTES)


def _ffn_kernel(*refs, d_ff, mix_out):
    if mix_out:
        omla_ref, ofox_ref, y_ref, wo_ref, gmix_ref, gpre_ref, wgu_ref, wd_ref, gpost_ref, o_ref, acc_ref = refs
        half = omla_ref.shape[1]
        mixed = (_dot(omla_ref[...].astype(BF16), wo_ref[:half, :])
                 + _dot(ofox_ref[...].astype(BF16), wo_ref[half:, :]))
        x = y_ref[...] + _rms(mixed, gmix_ref[...])
        o_ref[...] = x
        x_ref = o_ref
    else:
        x_ref, gpre_ref, wgu_ref, wd_ref, gpost_ref, o_ref, acc_ref = refs
        x = x_ref[...]
    h = _rms(x, gpre_ref[...]).astype(BF16)
    for c in range(d_ff // FF_CHUNK):
        lo = c * FF_CHUNK
        a = _dot(h, wgu_ref[:, lo:lo + FF_CHUNK])
        b = _dot(h, wgu_ref[:, d_ff + lo:d_ff + lo + FF_CHUNK])
        act = (a * jax.nn.sigmoid(a) * b).astype(BF16)
        part = _dot(act, wd_ref[lo:lo + FF_CHUNK, :])
        if c == 0:
            acc_ref[...] = part
        else:
            acc_ref[...] += part
    o_ref[...] = x_ref[...] + 0.5 * _rms(acc_ref[...], gpost_ref[...])


def _ffn_half(x, g_pre, w_gu, w_down, g_post, mix=None):
    n, d = x.shape
    d_ff = w_down.shape[0]
    tm = min(FFN_ROW_TILE, n)
    rows = lambda c: pl.BlockSpec((tm, c), lambda i: (i, 0))
    ins, in_specs = [x], [rows(d)]
    if mix is not None:
        o_mla, o_fox, w_o, g_mix = mix
        ins = [o_mla, o_fox, x, w_o, g_mix]
        in_specs = [rows(o_mla.shape[1]), rows(o_fox.shape[1]), rows(d), _resident(w_o.shape), _resident((1, d))]
    return pl.pallas_call(
        functools.partial(_ffn_kernel, d_ff=d_ff, mix_out=mix is not None),
        grid=(n // tm,),
        in_specs=in_specs + [_resident((1, d)), _resident((d, 2 * d_ff)), _resident((d_ff, d)), _resident((1, d))],
        out_specs=rows(d),
        out_shape=jax.ShapeDtypeStruct((n, d), F32),
        scratch_shapes=[pltpu.VMEM((tm, d), F32)],
        compiler_params=_params("parallel"),
        name="ffn_half" if mix is None else "mix_out_ffn_half",
    )(*ins, g_pre, w_gu, w_down, g_post)


def _log_sigmoid(x):
    return -(jnp.maximum(-x, 0.0) + jnp.log1p(jnp.exp(-jnp.abs(x))))


def _mix_in_kernel(*refs, q_lora, kv_lora, transposed):
    if transposed:
        (y_ref, gpre_ref, wn_ref, wt_ref, gq_ref, wq_ref, gkv_ref, wuk_ref, b_ref, bt_ref,
         cosq_ref, sinq_ref, cost_ref, sint_ref,
         ckv_ref, q_ref, fq_ref, lf_ref, kfull_ref,
         fkt_ref, fktb_ref, fvt_ref, fvtb_ref, krt_ref, lft_ref) = refs
    else:
        (y_ref, gpre_ref, wn_ref, gq_ref, wq_ref, gkv_ref, wuk_ref, b_ref,
         cosq_ref, sinq_ref,
         ckv_ref, q_ref, fq_ref, lf_ref, fk_ref, fv_ref, kr_ref) = refs
    c_fox = H_FOX * D_FOX
    o_kv = q_lora
    o_fq = o_kv + kv_lora
    o_tail = o_fq + c_fox
    mla_scale = MLA_SCALE * LOG2E if transposed else MLA_SCALE
    fox_scale = FOX_SCALE * LOG2E if transposed else FOX_SCALE

    h = _rms(y_ref[0], gpre_ref[...]).astype(BF16)
    pn = _dot(h, wn_ref[...])

    ckv = _rms(pn[:, o_kv:o_fq], gkv_ref[...])
    ckv_ref[0] = ckv

    tail = pn[:, o_tail:o_tail + LANES]
    lane = lax.broadcasted_iota(jnp.int32, tail.shape, 1)
    partner = jnp.where(lane < HALF_ROPE, pltpu.roll(tail, LANES - HALF_ROPE, 1), pltpu.roll(tail, HALF_ROPE, 1))
    kr = (tail * cosq_ref[0][:, :LANES] + partner * sinq_ref[0][:, :LANES])[:, :D_ROPE]
    lf_ref[0] = _log_sigmoid(tail[:, D_ROPE:D_ROPE + H_FOX] + b_ref[...])

    qn = _rms(pn[:, :q_lora], gq_ref[...]).astype(BF16)
    q = _dot(qn, wq_ref[...])
    n_nope = H_MLA * D_NOPE
    qr = q[:, n_nope:]
    width = H_MLA * D_ROPE
    lane = lax.broadcasted_iota(jnp.int32, qr.shape, 1)
    partner = jnp.where((lane % D_ROPE) < HALF_ROPE,
                        pltpu.roll(qr, width - HALF_ROPE, 1), pltpu.roll(qr, HALF_ROPE, 1))
    qr = (qr * cosq_ref[0] + partner * sinq_ref[0]) * mla_scale
    for hh in range(H_MLA):
        qh = q[:, hh * D_NOPE:(hh + 1) * D_NOPE].astype(BF16)
        q_lat = _dot(qh, wuk_ref[hh]) * mla_scale
        q_ref[0, hh] = jnp.concatenate([q_lat, qr[:, hh * D_ROPE:(hh + 1) * D_ROPE]], axis=1).astype(BF16)

    fq = pn[:, o_fq:o_tail] * fox_scale
    for hh in range(H_FOX):
        fq_ref[0, hh] = fq[:, hh * D_FOX:(hh + 1) * D_FOX].astype(BF16)

    if transposed:
        kfull_ref[0] = jnp.concatenate([ckv, kr], axis=1).astype(BF16)
        pt = _dot_nt(wt_ref[...], h)
        fkt = pt[:c_fox]
        fvt = pt[c_fox:2 * c_fox]
        fkt_ref[0] = fkt
        fktb_ref[0, 0] = fkt.astype(BF16)
        fvt_ref[0] = fvt
        tm = fvt.shape[1]
        ones_row = (lax.broadcasted_iota(jnp.int32, (FOX_V_ROWS - D_FOX, tm), 0) == 0).astype(F32)
        for hh in range(H_FOX):
            fvtb_ref[0, 0, hh] = jnp.concatenate([fvt[hh * D_FOX:(hh + 1) * D_FOX], ones_row], axis=0).astype(BF16)
        o_kr = 2 * c_fox
        x1 = pt[o_kr:o_kr + HALF_ROPE]
        x2 = pt[o_kr + HALF_ROPE:o_kr + D_ROPE]
        cos, sin = cost_ref[...], sint_ref[...]
        krt_ref[0] = jnp.concatenate([x1 * cos - x2 * sin, x2 * cos + x1 * sin], axis=0)
        lft_ref[0] = _log_sigmoid(pt[o_kr + D_ROPE:o_kr + D_ROPE + H_FOX] + bt_ref[...])
    else:
        o_fk = o_tail + LANES
        fk_ref[0] = pn[:, o_fk:o_fk + c_fox]
        fv_ref[0] = pn[:, o_fk + c_fox:o_fk + 2 * c_fox]
        kr_ref[0] = kr


def _mix_in(y, prm, rope, *, transposed):
    bsz, t, d = y.shape
    tm = min(ATTN_TILE, t)
    q_lora = prm["g_q_a"].shape[1]
    kv_lora = prm["g_kv_a"].shape[1]
    c_fox = H_FOX * D_FOX
    grid = (bsz, t // tm)
    tok = lambda c: pl.BlockSpec((1, tm, c), lambda b, i: (b, i, 0))
    head = lambda c: pl.BlockSpec((1, H_MLA, tm, c), lambda b, i: (b, 0, i, 0))
    chan = lambda c: pl.BlockSpec((1, c, tm), lambda b, i: (b, 0, i))
    sds = jax.ShapeDtypeStruct
    wn = prm["w_in_n_t"] if transposed else prm["w_in_n"]
    ins = [y, prm["g_pre_mix"], wn]
    in_specs = [tok(d), _resident((1, d)), _resident(wn.shape)]
    if transposed:
        ins.append(prm["w_in_t"])
        in_specs.append(_resident(prm["w_in_t"].shape))
    ins += [prm["g_q_a"], prm["w_q"], prm["g_kv_a"], prm["w_uk_t"], prm["b_forget"]]
    in_specs += [_resident((1, q_lora)), _resident(prm["w_q"].shape), _resident((1, kv_lora)),
                 _resident(prm["w_uk_t"].shape), _resident((1, H_FOX))]
    if transposed:
        ins.append(prm["b_forget_t"])
        in_specs.append(_resident((H_FOX, 1)))
    rw = H_MLA * D_ROPE
    ins += [rope["cos_q"], rope["sin_q"]]
    in_specs += [pl.BlockSpec((1, tm, rw), lambda b, i: (0, i, 0))] * 2
    if transposed:
        ins += [rope["cos_t"], rope["sin_t"]]
        in_specs += [pl.BlockSpec((HALF_ROPE, tm), lambda b, i: (0, i))] * 2
    qk_dim = kv_lora + D_ROPE
    names = ["ckv", "q", "fq", "lf"]
    out_shape = [sds((bsz, t, kv_lora), F32), sds((bsz, H_MLA, t, qk_dim), BF16),
                 sds((bsz, H_FOX, t, D_FOX), BF16), sds((bsz, t, H_FOX), F32)]
    out_specs = [tok(kv_lora), head(qk_dim), head(D_FOX), tok(H_FOX)]
    if transposed:
        names += ["k_full", "fk_t", "fk_tb", "fv_t", "fv_tb", "kr_t", "lf_t"]
        nk = t // tm
        out_shape += [sds((bsz, t, qk_dim), BF16), sds((bsz, c_fox, t), F32), sds((bsz, nk, c_fox, tm), BF16),
                      sds((bsz, c_fox, t), F32), sds((bsz, nk, H_FOX, FOX_V_ROWS, tm), BF16),
                      sds((bsz, D_ROPE, t), F32), sds((bsz, H_FOX, t), F32)]
        out_specs += [tok(qk_dim), chan(c_fox), pl.BlockSpec((1, 1, c_fox, tm), lambda b, i: (b, i, 0, 0)),
                      chan(c_fox), pl.BlockSpec((1, 1, H_FOX, FOX_V_ROWS, tm), lambda b, i: (b, i, 0, 0, 0)),
                      chan(D_ROPE), chan(H_FOX)]
    else:
        names += ["fk", "fv", "kr"]
        out_shape += [sds((bsz, t, c_fox), F32), sds((bsz, t, c_fox), F32), sds((bsz, t, D_ROPE), F32)]
        out_specs += [tok(c_fox), tok(c_fox), tok(D_ROPE)]
    outs = pl.pallas_call(
        functools.partial(_mix_in_kernel, q_lora=q_lora, kv_lora=kv_lora, transposed=transposed),
        grid=grid, in_specs=in_specs, out_specs=out_specs, out_shape=out_shape,
        compiler_params=_params("parallel", "parallel"),
        name="mix_in_prompt" if transposed else "mix_in_sample",
    )(*ins)
    return dict(zip(names, outs))


def _cumsum_kernel(lf_ref, lft_ref, f_ref, ft_ref):
    t = lf_ref.shape[1]
    nb = t // CUMSUM_BLOCK
    r = lax.broadcasted_iota(jnp.int32, (CUMSUM_BLOCK, CUMSUM_BLOCK), 0)
    c = lax.broadcasted_iota(jnp.int32, (CUMSUM_BLOCK, CUMSUM_BLOCK), 1)
    tri = (c <= r).astype(F32)
    carry = jnp.zeros((1, H_FOX), F32)
    carry_t = jnp.zeros((H_FOX, 1), F32)
    for i in range(nb):
        lo = i * CUMSUM_BLOCK
        blk = jnp.dot(tri, lf_ref[0, lo:lo + CUMSUM_BLOCK, :], preferred_element_type=F32,
                      precision=lax.Precision.HIGHEST) + carry
        f_ref[0, lo:lo + CUMSUM_BLOCK, :] = blk * LOG2E
        carry = blk[CUMSUM_BLOCK - 1:CUMSUM_BLOCK, :]
        blk_t = lax.dot_general(lft_ref[0, :, lo:lo + CUMSUM_BLOCK], tri, (((1,), (1,)), ((), ())),
                                preferred_element_type=F32, precision=lax.Precision.HIGHEST) + carry_t
        per_tile = ft_ref.shape[3] // CUMSUM_BLOCK
        lo_t = (i % per_tile) * CUMSUM_BLOCK
        ft_ref[0, i // per_tile, :, lo_t:lo_t + CUMSUM_BLOCK] = blk_t * LOG2E
        carry_t = blk_t[:, CUMSUM_BLOCK - 1:CUMSUM_BLOCK]


def _fox_cumsum(lf, lf_t, tile):
    bsz, t, hh = lf.shape
    ft_shape = (bsz, t // tile, hh, tile)
    return pl.pallas_call(
        _cumsum_kernel,
        grid=(bsz,),
        in_specs=[pl.BlockSpec((1, t, hh), lambda b: (b, 0, 0)), pl.BlockSpec((1, hh, t), lambda b: (b, 0, 0))],
        out_specs=[pl.BlockSpec((1, t, hh), lambda b: (b, 0, 0)),
                   pl.BlockSpec((1,) + ft_shape[1:], lambda b: (b, 0, 0, 0))],
        out_shape=[jax.ShapeDtypeStruct(lf.shape, F32), jax.ShapeDtypeStruct(ft_shape, F32)],
        compiler_params=_params("parallel"),
        name="fox_cumsum",
    )(lf, lf_t)


def _diagonal_visible(tile):
    return lax.broadcasted_iota(jnp.int32, (tile, tile), 1) <= lax.broadcasted_iota(jnp.int32, (tile, tile), 0)


def _for_causal_key_tiles(qi, key_tile):
    def body(ki, carry):
        key_tile(ki, False)
        return carry

    lax.fori_loop(0, qi, body, 0)
    key_tile(qi, True)


def _row_max(s):
    mx = s[:, :LANES]
    for lo in range(LANES, s.shape[1], LANES):
        mx = jnp.maximum(mx, s[:, lo:lo + LANES])
    return jnp.max(mx, axis=-1, keepdims=True)


def _lane_tile(x, width):
    if width <= LANES:
        return x[:, :width]
    return jnp.concatenate([x] * (width // LANES), axis=1)


def _per_batch(shape):
    nd = len(shape)
    return pl.BlockSpec((1,) + tuple(shape[1:]), lambda b, i: (b,) + (0,) * (nd - 1), pipeline_mode=pl.Buffered(1))


def _mla_prefill_kernel(q_ref, k_ref, wuv_ref, o_ref, m_ref, l_ref, alpha_ref, acc_ref, s_ref, p_ref, *, tile, c):
    m_ref[...] = jnp.full(m_ref.shape, NEG_BIG, F32)
    l_ref[...] = jnp.zeros(l_ref.shape, F32)
    acc_ref[...] = jnp.zeros(acc_ref.shape, F32)

    def key_tile(ki, masked):
        k_tile = k_ref.at[0, pl.ds(pl.multiple_of(ki * tile, tile), tile), :]
        visible = _diagonal_visible(tile) if masked else None

        def logits(hh):
            s = _dot_nt(q_ref[0, hh], k_tile[...])
            if masked:
                s = jnp.where(visible, s, NEG_BIG)
            s_ref[hh] = s
            m_old = m_ref[hh]
            m_new = jnp.maximum(m_old, _row_max(s))
            alpha_ref[hh] = jnp.exp2(m_old - m_new)
            m_ref[hh] = m_new

        logits(0)
        for hh in range(H_MLA):
            if hh + 1 < H_MLA:
                logits(hh + 1)
            for r0 in range(0, tile, SOFTMAX_ROWS):
                rows = slice(r0, r0 + SOFTMAX_ROWS)
                m_new = m_ref[hh, rows]
                p_sum = None
                for lo in range(0, tile, LANES):
                    p = jnp.exp2(s_ref[hh, rows, lo:lo + LANES] - m_new)
                    p_ref[hh, rows, lo:lo + LANES] = p.astype(BF16)
                    p_sum = p if p_sum is None else p_sum + p
                l_ref[hh, rows] = alpha_ref[hh, rows] * l_ref[hh, rows] + p_sum
            acc_ref[hh] = _lane_tile(alpha_ref[hh], c) * acc_ref[hh] + _dot(p_ref[hh], k_tile[:, :c])

    _for_causal_key_tiles(pl.program_id(1), key_tile)

    outs = []
    for hh in range(H_MLA):
        o_lat = acc_ref[hh] / jnp.sum(l_ref[hh], axis=-1, keepdims=True)
        outs.append(_dot(o_lat.astype(BF16), wuv_ref[hh]))
    o_ref[0] = jnp.concatenate(outs, axis=-1).astype(BF16)


def _mla_prefill(q, k_full, w_uv):
    bsz, _, t, qk_dim = q.shape
    c = qk_dim - D_ROPE
    tile = min(ATTN_TILE, t)
    return pl.pallas_call(
        functools.partial(_mla_prefill_kernel, tile=tile, c=c),
        grid=(bsz, t // tile),
        in_specs=[
            pl.BlockSpec((1, H_MLA, tile, qk_dim), lambda b, i: (b, 0, i, 0)),
            _per_batch(k_full.shape),
            _resident(w_uv.shape),
        ],
        out_specs=pl.BlockSpec((1, tile, H_MLA * D_V_MLA), lambda b, i: (b, i, 0)),
        out_shape=jax.ShapeDtypeStruct((bsz, t, H_MLA * D_V_MLA), BF16),
        scratch_shapes=[pltpu.VMEM((H_MLA, tile, LANES), F32), pltpu.VMEM((H_MLA, tile, LANES), F32),
                        pltpu.VMEM((H_MLA, tile, LANES), F32), pltpu.VMEM((H_MLA, tile, c), F32),
                        pltpu.VMEM((H_MLA, tile, tile), F32), pltpu.VMEM((H_MLA, tile, tile), BF16)],
        compiler_params=_params("parallel", "parallel"),
        name="mla_prefill",
    )(q, k_full, w_uv)


def _fox_prefill_kernel(q_ref, kt_ref, vt_ref, f_ref, ft_ref, o_ref,
                        m_ref, alpha_ref, cq_ref, shift_ref, acc_ref, s_ref, p_ref, *, tile):
    m_ref[...] = jnp.full(m_ref.shape, NEG_BIG, F32)
    acc_ref[...] = jnp.zeros(acc_ref.shape, F32)
    f_q = f_ref[0]
    for hh in range(H_FOX):
        cq_ref[hh] = jnp.broadcast_to(f_q[:, hh:hh + 1], (tile, LANES))

    def key_tile(ki, masked):
        visible = _diagonal_visible(tile) if masked else None

        def logits(hh):
            u = _dot(q_ref[0, hh], kt_ref[0, ki, hh * D_FOX:(hh + 1) * D_FOX, :]) - ft_ref[0, ki, hh:hh + 1, :]
            if masked:
                u = jnp.where(visible, u, NEG_BIG)
            s_ref[hh] = u
            cq = cq_ref[hh]
            m_old = m_ref[hh]
            m_new = jnp.maximum(m_old, _row_max(u) + cq)
            alpha_ref[hh] = jnp.exp2(m_old - m_new)
            m_ref[hh] = m_new
            shift_ref[hh] = cq - m_new

        logits(0)
        for hh in range(H_FOX):
            if hh + 1 < H_FOX:
                logits(hh + 1)
            for r0 in range(0, tile, SOFTMAX_ROWS):
                rows = slice(r0, r0 + SOFTMAX_ROWS)
                shift = shift_ref[hh, rows]
                for lo in range(0, tile, LANES):
                    p_ref[hh, rows, lo:lo + LANES] = jnp.exp2(s_ref[hh, rows, lo:lo + LANES] + shift).astype(BF16)
            acc_ref[hh] = (_lane_tile(alpha_ref[hh], FOX_V_ROWS) * acc_ref[hh]
                           + _dot_nt(p_ref[hh], vt_ref[0, ki, hh]))

    _for_causal_key_tiles(pl.program_id(1), key_tile)

    outs = []
    for hh in range(H_FOX):
        acc = acc_ref[hh]
        outs.append(acc[:, :D_FOX] / acc[:, D_FOX:D_FOX + 1])
    o_ref[0] = jnp.concatenate(outs, axis=-1).astype(BF16)


def _fox_prefill(fq, fk_tb, fv_tb, f, f_t):
    bsz, _, t, _ = fq.shape
    tile = fk_tb.shape[-1]
    c_fox = H_FOX * D_FOX
    return pl.pallas_call(
        functools.partial(_fox_prefill_kernel, tile=tile),
        grid=(bsz, t // tile),
        in_specs=[
            pl.BlockSpec((1, H_FOX, tile, D_FOX), lambda b, i: (b, 0, i, 0)),
            _per_batch(fk_tb.shape),
            _per_batch(fv_tb.shape),
            pl.BlockSpec((1, tile, H_FOX), lambda b, i: (b, i, 0)),
            _per_batch(f_t.shape),
        ],
        out_specs=pl.BlockSpec((1, tile, c_fox), lambda b, i: (b, i, 0)),
        out_shape=jax.ShapeDtypeStruct((bsz, t, c_fox), BF16),
        scratch_shapes=[pltpu.VMEM((H_FOX, tile, LANES), F32), pltpu.VMEM((H_FOX, tile, LANES), F32),
                        pltpu.VMEM((H_FOX, tile, LANES), F32), pltpu.VMEM((H_FOX, tile, LANES), F32),
                        pltpu.VMEM((H_FOX, tile, FOX_V_ROWS), F32),
                        pltpu.VMEM((H_FOX, tile, tile), F32), pltpu.VMEM((H_FOX, tile, tile), BF16)],
        compiler_params=_params("parallel", "parallel"),
        name="fox_prefill",
    )(fq, fk_tb, fv_tb, f, f_t)


def _suffix_sums(x):
    n = x.shape[1]
    lane = lax.broadcasted_iota(jnp.int32, x.shape, 1)
    y = x
    k = 1
    while k < n:
        y = y + jnp.where(lane + k < n, pltpu.roll(y, n - k, 1), 0.0)
        k *= 2
    return y


def _decode_kernel(pt_ref, qlat_ref, qrope_ref, qbd_ref, cnew_ref, krnew_ref, fknew_ref, fvnew_ref, lfnew_ref,
                   wuv_ref, lat_hbm, krt_hbm, fkt_hbm, fvt_hbm, lft_hbm, omla_ref, ofox_ref,
                   m1_ref, l1_ref, acc1_ref, m2_ref, l2_ref, acc2_ref, gq_ref, carry_ref,
                   lat_buf, krt_buf, fkt_buf, fvt_buf, lft_buf, sems, *, nps, n_pages, n_new):
    b_id = pl.program_id(0)
    c_id = pl.program_id(1)
    n_steps = pl.num_programs(1)
    g_id = b_id * n_steps + c_id
    slot = g_id % 2
    pairs = ((lat_hbm, lat_buf), (krt_hbm, krt_buf), (fkt_hbm, fkt_buf), (fvt_hbm, fvt_buf), (lft_hbm, lft_buf))

    def page_copies(b, c, slot_):
        base = b * n_pages + (n_steps - 1 - c) * nps
        copies = []
        for j in range(nps):
            page = pt_ref[base + j]
            for k, (hbm, buf) in enumerate(pairs):
                copies.append(pltpu.make_async_copy(hbm.at[page], buf.at[slot_, j], sems.at[slot_, k]))
        return copies

    @pl.when(g_id == 0)
    def _():
        for cp in page_copies(b_id, c_id, slot):
            cp.start()

    @pl.when(g_id + 1 < pl.num_programs(0) * n_steps)
    def _():
        wrap = c_id + 1 == n_steps
        for cp in page_copies(jnp.where(wrap, b_id + 1, b_id), jnp.where(wrap, 0, c_id + 1), 1 - slot):
            cp.start()

    for cp in page_copies(b_id, c_id, slot):
        cp.wait()

    rows = qlat_ref.shape[1]
    q_lat = qlat_ref[0]
    q_rope = qrope_ref[0]
    q_bd = qbd_ref[0]

    def online_update(m_ref, l_ref, acc_ref, s, value_fn):
        m_old = m_ref[...]
        m_new = jnp.maximum(m_old, jnp.max(s, axis=-1, keepdims=True))
        alpha = jnp.exp(m_old - m_new)
        p = jnp.exp(s - m_new)
        l_ref[...] = alpha * l_ref[...] + jnp.sum(p, axis=-1, keepdims=True)
        acc_ref[...] = alpha * acc_ref[...] + value_fn(p.astype(BF16))
        m_ref[...] = m_new

    def add_forget_bias(s, g_keys):
        gq = gq_ref[...]
        return jnp.concatenate(
            [s[t * H_FOX:(t + 1) * H_FOX] + (g_keys - gq[:, t:t + 1]) for t in range(n_new)], axis=0)

    @pl.when(c_id == 0)
    def _():
        lf_new = lfnew_ref[0]
        incl = _suffix_sums(lf_new)
        g_new = incl - lf_new
        gq_ref[...] = g_new
        carry_ref[...] = jnp.broadcast_to(incl[:, 0:1], carry_ref.shape)
        row_tok = lax.broadcasted_iota(jnp.int32, (rows, NEW_PAD), 0) // H_FOX
        key = lax.broadcasted_iota(jnp.int32, (rows, NEW_PAD), 1)
        visible = key <= row_tok
        for ref in (m1_ref, m2_ref):
            ref[...] = jnp.full(ref.shape, NEG_BIG, F32)
        for ref in (l1_ref, l2_ref, acc1_ref, acc2_ref):
            ref[...] = jnp.zeros(ref.shape, F32)
        c_new = cnew_ref[0].astype(BF16)
        s1 = _dot_nt(q_lat, c_new) + _dot_nt(q_rope, krnew_ref[0].astype(BF16))
        online_update(m1_ref, l1_ref, acc1_ref, jnp.where(visible, s1, NEG_BIG), lambda p: _dot(p, c_new))
        fv_new = fvnew_ref[0].astype(BF16)
        s2 = add_forget_bias(_dot_nt(q_bd, fknew_ref[0].astype(BF16)), g_new[:, :NEW_PAD])
        online_update(m2_ref, l2_ref, acc2_ref, jnp.where(visible, s2, NEG_BIG), lambda p: _dot(p, fv_new))

    def along_lanes(buf):
        return jnp.concatenate([buf[slot, j] for j in range(nps)], axis=1)

    lat = jnp.concatenate([lat_buf[slot, j] for j in range(nps)], axis=0).astype(BF16)
    krt = along_lanes(krt_buf).astype(BF16)
    s1 = _dot_nt(q_lat, lat) + _dot(q_rope, krt)
    online_update(m1_ref, l1_ref, acc1_ref, s1, lambda p: _dot(p, lat))

    lf = along_lanes(lft_buf)
    incl = _suffix_sums(lf)
    carry = carry_ref[:, 0:1]
    g_keys = carry + (incl - lf)
    carry_ref[...] = jnp.broadcast_to(carry + incl[:, 0:1], carry_ref.shape)
    fkt = along_lanes(fkt_buf).astype(BF16)
    fvt = along_lanes(fvt_buf).astype(BF16)
    s2 = add_forget_bias(_dot(q_bd, fkt), g_keys)
    online_update(m2_ref, l2_ref, acc2_ref, s2, lambda p: _dot_nt(p, fvt))

    @pl.when(c_id == pl.num_programs(1) - 1)
    def _():
        width = H_MLA * D_V_MLA
        row_head = lax.broadcasted_iota(jnp.int32, (rows, width), 0) % H_MLA
        col_head = lax.broadcasted_iota(jnp.int32, (rows, width), 1) // D_V_MLA
        own = row_head == col_head
        o_lat = (acc1_ref[...] / l1_ref[...]).astype(BF16)
        o1 = jnp.where(own, _dot(o_lat, wuv_ref[...]), 0.0)
        o2 = jnp.where(own, acc2_ref[...] / l2_ref[...], 0.0)
        omla_ref[0] = jnp.sum(o1.reshape(n_new, H_MLA, width), axis=1)
        ofox_ref[0] = jnp.sum(o2.reshape(n_new, H_FOX, width), axis=1)


def _paged_decode(page_table, q_lat, q_rope, q_bd, c_new, kr_new, fk_new, fv_new, lf_new_t, w_uv_all,
                  lat_pages, krt_pages, fkt_pages, fvt_pages, lft_pages):
    bsz, rows, c = q_lat.shape
    n_new = rows // H_MLA
    n_pages = page_table.shape[1]
    page = lat_pages.shape[1]
    nps = min(PAGES_PER_STEP, n_pages)
    n_steps = n_pages // nps
    c_fox = H_FOX * D_FOX
    pt_flat = page_table.reshape(-1)

    per_b = lambda shape: pl.BlockSpec((1,) + shape, lambda b, s, pt: (b, 0, 0))
    in_specs = [per_b((rows, c)), per_b((rows, D_ROPE)), per_b((rows, c_fox)),
                per_b((NEW_PAD, c)), per_b((NEW_PAD, D_ROPE)), per_b((NEW_PAD, c_fox)), per_b((NEW_PAD, c_fox)),
                per_b((H_FOX, LANES)), _resident(w_uv_all.shape)] + [pl.BlockSpec(memory_space=pl.ANY)] * 5
    ins = [q_lat, q_rope, q_bd, c_new, kr_new, fk_new, fv_new, lf_new_t, w_uv_all,
           lat_pages, krt_pages, fkt_pages, fvt_pages, lft_pages]
    page_bufs = [pltpu.VMEM((2, nps) + shape, F32)
                 for shape in ((page, c), (D_ROPE, page), (c_fox, page), (c_fox, page), (H_FOX, page))]
    grid_spec = pltpu.PrefetchScalarGridSpec(
        num_scalar_prefetch=1,
        grid=(bsz, n_steps),
        in_specs=in_specs,
        out_specs=[pl.BlockSpec((1, n_new, c_fox), lambda b, s, pt: (b, 0, 0))] * 2,
        scratch_shapes=[pltpu.VMEM((rows, 1), F32), pltpu.VMEM((rows, 1), F32), pltpu.VMEM((rows, c), F32),
                        pltpu.VMEM((rows, 1), F32), pltpu.VMEM((rows, 1), F32), pltpu.VMEM((rows, c_fox), F32),
                        pltpu.VMEM((H_FOX, LANES), F32), pltpu.VMEM((H_FOX, LANES), F32)]
        + page_bufs + [pltpu.SemaphoreType.DMA((2, len(page_bufs)))],
    )
    return pl.pallas_call(
        functools.partial(_decode_kernel, nps=nps, n_pages=n_pages, n_new=n_new),
        grid_spec=grid_spec,
        out_shape=[jax.ShapeDtypeStruct((bsz, n_new, c_fox), F32)] * 2,
        compiler_params=_params("arbitrary", "arbitrary"),
        name="paged_decode",
    )(pt_flat, *ins)


def _rope_tables(pos):
    freq = ROPE_THETA ** (-jnp.arange(HALF_ROPE, dtype=F32) * 2.0 / D_ROPE)
    ang = pos[:, None] * freq[None, :]
    cos, sin = jnp.cos(ang), jnp.sin(ang)
    cos_q = jnp.tile(jnp.concatenate([cos, cos], axis=1), (1, H_MLA))[None]
    sin_q = jnp.tile(jnp.concatenate([-sin, sin], axis=1), (1, H_MLA))[None]
    return {"cos_q": cos_q, "sin_q": sin_q, "cos_t": cos.T, "sin_t": sin.T}


def _prepare_weights(layer, w):
    g = lambda name: w[name][layer]
    q_lora = g("g_q_a").shape[0]
    kv_lora = g("g_kv_a").shape[0]
    c_fox = H_FOX * D_FOX
    w_in = g("w_in")
    o1 = q_lora
    o2 = o1 + kv_lora
    o3 = o2 + D_ROPE
    o4, o5, o6 = o3 + c_fox, o3 + 2 * c_fox, o3 + 3 * c_fox
    qa, kva, kr = w_in[:, :o1], w_in[:, o1:o2], w_in[:, o2:o3]
    fq, fk, fv, fl = w_in[:, o3:o4], w_in[:, o4:o5], w_in[:, o5:o6], w_in[:, o6:]
    pad = jnp.zeros((w_in.shape[0], LANES - D_ROPE - H_FOX), w_in.dtype)
    w_q_b = g("w_q_b").reshape(q_lora, H_MLA, D_NOPE + D_ROPE)
    w_kv_b = g("w_kv_b").reshape(kv_lora, H_MLA, D_NOPE + D_V_MLA)
    prm = {
        "w_in_n": jnp.concatenate([qa, kva, fq, kr, fl, pad, fk, fv], axis=1).astype(BF16),
        "w_in_n_t": jnp.concatenate([qa, kva, fq, kr, fl, pad], axis=1).astype(BF16),
        "w_in_t": jnp.concatenate([fk, fv, kr, fl], axis=1).T.astype(BF16),
        "w_q": jnp.concatenate([w_q_b[..., :D_NOPE].reshape(q_lora, -1),
                                w_q_b[..., D_NOPE:].reshape(q_lora, -1)], axis=1).astype(BF16),
        "w_uk_t": jnp.transpose(w_kv_b[..., :D_NOPE], (1, 2, 0)).astype(BF16),
        "w_uv": jnp.transpose(w_kv_b[..., D_NOPE:], (1, 0, 2)).astype(BF16),
        "w_uv_all": w_kv_b[..., D_NOPE:].reshape(kv_lora, -1).astype(BF16),
        "w_o": g("w_o").astype(BF16),
        "b_forget": g("b_forget")[None, :],
        "b_forget_t": g("b_forget")[:, None],
    }
    for name in ("g_pre_mix", "g_q_a", "g_kv_a", "g_post_mix", "g_pre_ffn1", "g_post_ffn1", "g_pre_ffn2", "g_post_ffn2"):
        prm[name] = g(name)[None, :]
    for tag in ("ffn1", "ffn2"):
        prm["w_gu_" + tag] = g("w_gu_" + tag).astype(BF16)
        prm["w_down_" + tag] = g("w_down_" + tag).astype(BF16)
    return prm


def _ffn(x, prm, tag, mixer_out=None):
    mix = None if mixer_out is None else mixer_out + (prm["w_o"], prm["g_post_mix"])
    return _ffn_half(x, prm["g_pre_" + tag], prm["w_gu_" + tag], prm["w_down_" + tag], prm["g_post_" + tag], mix)


def _prompt_layer(x, prm, rope):
    bsz, t, d = x.shape
    y1 = _ffn(x.reshape(bsz * t, d), prm, "ffn1")
    mi = _mix_in(y1.reshape(bsz, t, d), prm, rope, transposed=True)
    f, f_t = _fox_cumsum(mi["lf"], mi["lf_t"], mi["fk_tb"].shape[-1])
    o_mla = _mla_prefill(mi["q"], mi["k_full"], prm["w_uv"])
    o_fox = _fox_prefill(mi["fq"], mi["fk_tb"], mi["fv_tb"], f, f_t)
    out = _ffn(y1, prm, "ffn2", (o_mla.reshape(bsz * t, -1), o_fox.reshape(bsz * t, -1))).reshape(bsz, t, d)
    rows = (mi["ckv"],
            jnp.transpose(mi["kr_t"], (0, 2, 1)),
            jnp.transpose(mi["fk_t"].reshape(bsz, H_FOX, D_FOX, t), (0, 3, 1, 2)),
            jnp.transpose(mi["fv_t"].reshape(bsz, H_FOX, D_FOX, t), (0, 3, 1, 2)),
            jnp.transpose(mi["lf_t"], (0, 2, 1)))
    return out, rows


def _sample_layer(x, prm, rope, caches, page_table):
    bsz, n_new, d = x.shape
    n_tok = bsz * n_new
    c_fox = H_FOX * D_FOX
    y1 = _ffn(x.reshape(n_tok, d), prm, "ffn1")
    mi = _mix_in(y1.reshape(1, n_tok, d), prm, rope, transposed=False)

    def per_batch(a):
        return jnp.transpose(a[0].reshape(a.shape[1], bsz, n_new, a.shape[-1]), (1, 2, 0, 3)).reshape(
            bsz, n_new * a.shape[1], a.shape[-1])

    q = per_batch(mi["q"])
    kv_lora = q.shape[-1] - D_ROPE
    q_lat, q_rope = q[..., :kv_lora], q[..., kv_lora:]
    fq = per_batch(mi["fq"]).reshape(bsz, n_new, H_FOX, 1, D_FOX)
    eye = jnp.eye(H_FOX, dtype=fq.dtype)[None, None, :, :, None]
    q_bd = (fq * eye).reshape(bsz, n_new * H_FOX, c_fox)

    def new_rows(a):
        a = a.reshape(bsz, n_new, a.shape[-1])
        return jnp.pad(a, ((0, 0), (0, NEW_PAD - n_new), (0, 0)))

    lf_new_t = jnp.pad(jnp.transpose(mi["lf"].reshape(bsz, n_new, H_FOX), (0, 2, 1)),
                       ((0, 0), (0, 0), (0, LANES - n_new)))
    lat, kr, fk, fv, lf = caches
    n_pool, page = lat.shape[0], lat.shape[1]
    krt_pages = jnp.transpose(kr, (0, 2, 1))
    fkt_pages = jnp.transpose(fk, (0, 2, 3, 1)).reshape(n_pool, c_fox, page)
    fvt_pages = jnp.transpose(fv, (0, 2, 3, 1)).reshape(n_pool, c_fox, page)
    lft_pages = jnp.transpose(lf, (0, 2, 1))
    o_mla, o_fox = _paged_decode(page_table, q_lat, q_rope, q_bd, new_rows(mi["ckv"]), new_rows(mi["kr"]),
                                 new_rows(mi["fk"]), new_rows(mi["fv"]), lf_new_t, prm["w_uv_all"],
                                 lat, krt_pages, fkt_pages, fvt_pages, lft_pages)
    out = _ffn(y1, prm, "ffn2", (o_mla.reshape(n_tok, -1), o_fox.reshape(n_tok, -1))).reshape(bsz, n_new, d)
    rows = (mi["ckv"].reshape(bsz, n_new, -1), mi["kr"].reshape(bsz, n_new, -1),
            mi["fk"].reshape(bsz, n_new, H_FOX, D_FOX), mi["fv"].reshape(bsz, n_new, H_FOX, D_FOX),
            mi["lf"].reshape(bsz, n_new, H_FOX))
    return out, rows


def kernel(x_prompt, x_sample, cache_mla_latent, cache_mla_krope, cache_fox_k, cache_fox_v, cache_fox_logf, page_table, g_pre_ffn1, w_gu_ffn1, w_down_ffn1, g_post_ffn1, g_pre_mix, w_in, g_q_a, w_q_b, g_kv_a, w_kv_b, b_forget, w_o, g_post_mix, g_pre_ffn2, w_gu_ffn2, w_down_ffn2, g_post_ffn2):
    weights = dict(g_pre_ffn1=g_pre_ffn1, w_gu_ffn1=w_gu_ffn1, w_down_ffn1=w_down_ffn1, g_post_ffn1=g_post_ffn1,
                   g_pre_mix=g_pre_mix, w_in=w_in, g_q_a=g_q_a, w_q_b=w_q_b, g_kv_a=g_kv_a, w_kv_b=w_kv_b,
                   b_forget=b_forget, w_o=w_o, g_post_mix=g_post_mix, g_pre_ffn2=g_pre_ffn2, w_gu_ffn2=w_gu_ffn2,
                   w_down_ffn2=w_down_ffn2, g_post_ffn2=g_post_ffn2)
    depth = w_in.shape[0]
    seq = x_prompt.shape[1]
    n_new = x_sample.shape[1]
    past_len = page_table.shape[1] * cache_mla_latent.shape[2]
    rope_p = _rope_tables(jnp.arange(seq, dtype=F32))
    rope_s = _rope_tables(jnp.tile(past_len + jnp.arange(n_new, dtype=F32), x_sample.shape[0]))
    yp, ys = x_prompt, x_sample
    p_rows = [[] for _ in range(5)]
    s_rows = [[] for _ in range(5)]
    for layer in range(depth):
        prm = _prepare_weights(layer, weights)
        yp, rows_p = _prompt_layer(yp, prm, rope_p)
        caches = (cache_mla_latent[layer], cache_mla_krope[layer], cache_fox_k[layer], cache_fox_v[layer],
                  cache_fox_logf[layer])
        ys, rows_s = _sample_layer(ys, prm, rope_s, caches, page_table)
        for i in range(5):
            p_rows[i].append(rows_p[i])
            s_rows[i].append(rows_s[i])
    return (yp, ys) + tuple(jnp.stack(r, axis=0) for r in p_rows) + tuple(jnp.stack(r, axis=0) for r in s_rows)
```
